```python
import math
import jax
import jax.numpy as jnp
from jax import lax
import numpy as np

D_MODEL = 2048
BATCH = 1
SEQ = 8192
DEPTH = 4

GRID_W = 64
CTX_LEN = 256
EPS = 1e-6
N_MOD = 9
D_FF = 5632

A_HEADS = 8
A_KV_HEADS = 2
A_HEAD_DIM = 128
A_BLOCK = 128
ROPE_THETA = 10000.0
A_WIDTH = A_HEADS * A_HEAD_DIM
A_KV_WIDTH = A_KV_HEADS * A_HEAD_DIM

B_HEADS = 4
B_DK = 128
B_DV = 256
B_RANK = 16
B_TAU = 16.0
B_CHUNK = 64
B_QK_WIDTH = B_HEADS * B_DK
B_WIDTH = B_HEADS * B_DV

C_HEADS = 16
C_HEAD_DIM = 64
C_GROUPS = 2
C_HEADS_PER_GROUP = C_HEADS // C_GROUPS
C_STATE = 128
C_CONV = 5
C_CHUNK = 128
C_INNER = C_HEADS * C_HEAD_DIM
C_XBC = C_INNER + 2 * C_GROUPS * C_STATE

N_BRANCH = 3
MIX_WIDTH = A_WIDTH + B_WIDTH + C_INNER
IN_SIZES = (A_WIDTH, A_KV_WIDTH, A_KV_WIDTH, B_QK_WIDTH, B_QK_WIDTH, B_WIDTH, B_WIDTH, 2 * B_RANK, C_XBC, C_INNER, 2 * C_HEADS, N_BRANCH * D_MODEL)
IN_COLS = sum(IN_SIZES)

kernel_name = "hybrid_dit_gqa_gla_ssd_prefix"


def rms_norm(x, g):
    xf = x.astype(jnp.float32)
    y = xf * lax.rsqrt(jnp.mean(xf * xf, axis=-1, keepdims=True) + EPS)
    return (y * g.astype(jnp.float32)).astype(x.dtype)


def modulate(x, g, shift, scale):
    return rms_norm(x, g) * (1.0 + scale) + shift


def swiglu(h, w_in, w_out):
    a, u = jnp.split(h @ w_in, 2, axis=-1)
    return (jax.nn.silu(a) * u) @ w_out


def split_points():
    return [int(v) for v in np.cumsum(IN_SIZES)[:-1]]


def flip_seq(t):
    return jnp.flip(t, axis=1)


def rope_tables(rows_n):
    rows = jnp.repeat(jnp.arange(rows_n), GRID_W)
    cols = jnp.tile(jnp.arange(GRID_W), rows_n)
    pos = jnp.stack([rows, cols], axis=-1).astype(jnp.float32)
    half = A_HEAD_DIM // 2
    inv = ROPE_THETA ** (-jnp.arange(0, half, 2, dtype=jnp.float32) / half)
    ang = pos[:, :, None] * inv
    return jnp.cos(ang), jnp.sin(ang)


def apply_axial_rope(x, cos, sin):
    b, s, h, d = x.shape
    xf = x.astype(jnp.float32).reshape(b, s, h, 2, 2, d // 4)
    x1, x2 = xf[..., 0, :], xf[..., 1, :]
    cs, sn = cos[None, :, None], sin[None, :, None]
    out = jnp.stack([x1 * cs - x2 * sn, x2 * cs + x1 * sn], axis=-2)
    return out.reshape(b, s, h, d).astype(x.dtype)


def centred_conv(u, w, b):
    pad = C_CONV // 2
    y = lax.conv_general_dilated(u, w[:, None, :].astype(u.dtype), window_strides=(1,), padding=[(pad, pad)], dimension_numbers=("NWC", "WIO", "NWC"), feature_group_count=u.shape[-1])
    return y + b


def attention_mixer(q_x, k_x, v_x, q_c, k_c, v_c, qk_g, rope, need_ctx):
    bsz, n_tok, _ = q_x.shape
    n_ctx = q_c.shape[1]
    grp = A_HEADS // A_KV_HEADS
    scale = A_HEAD_DIM ** -0.5

    def heads(q, k, v):
        t = q.shape[1]
        q = rms_norm(q.reshape(bsz, t, A_HEADS, A_HEAD_DIM), qk_g[0])
        k = rms_norm(k.reshape(bsz, t, A_KV_HEADS, A_HEAD_DIM), qk_g[1])
        return q, k, v.reshape(bsz, t, A_KV_HEADS, A_HEAD_DIM)

    def attend(q, k, v):
        s = jnp.einsum("bqhgd,bkhd->bhgqk", q, k).astype(jnp.float32) * scale
        pr = jax.nn.softmax(s, axis=-1).astype(v.dtype)
        return jnp.einsum("bhgqk,bkhd->bqhgd", pr, v)

    qx, kx, vx = heads(q_x, k_x, v_x)
    qc, kc, vc = heads(q_c, k_c, v_c)
    cos, sin = rope
    qx = apply_axial_rope(qx, cos, sin)
    kx = apply_axial_rope(kx, cos, sin)
    k_all = jnp.concatenate([kc, kx], axis=1)
    v_all = jnp.concatenate([vc, vx], axis=1)
    q_blocks = jnp.moveaxis(qx.reshape(bsz, n_tok // A_BLOCK, A_BLOCK, A_KV_HEADS, grp, A_HEAD_DIM), 1, 0)
    o_x = lax.map(lambda qb: attend(qb, k_all, v_all), q_blocks)
    o_x = jnp.moveaxis(o_x, 0, 1).reshape(bsz, n_tok, A_WIDTH)
    o_c = None
    if need_ctx:
        o_c = attend(qc.reshape(bsz, n_ctx, A_KV_HEADS, grp, A_HEAD_DIM), kc, vc).reshape(bsz, n_ctx, A_WIDTH)
    return o_x, o_c


def gla_chunked(q, k, v, log_a, s0, need_out):
    bsz, t, h, dk = q.shape
    n = t // B_CHUNK
    ch = lambda a: a.reshape(bsz, n, B_CHUNK, *a.shape[2:])
    q, k, v, log_a = ch(q), ch(k), ch(v), ch(log_a)
    b = jnp.cumsum(log_a, axis=2)
    b_last = b[:, :, -1]
    vf = v.astype(jnp.float32)
    k_end = k.astype(jnp.float32) * jnp.exp(b_last[:, :, None] - b)
    u = jnp.einsum("bnshd,bnshv->bnhdv", k_end, vf)
    decay = jnp.exp(b_last)

    def step(s, inp):
        d, uu = inp
        return d[..., None] * s + uu, s

    s_final, s_prev = lax.scan(step, s0, (jnp.moveaxis(decay, 1, 0), jnp.moveaxis(u, 1, 0)))
    if not need_out:
        return None, s_final
    s_prev = jnp.moveaxis(s_prev, 0, 1)
    qf = q.astype(jnp.float32) * jnp.exp(b)
    kf = k.astype(jnp.float32) * jnp.exp(-b)
    mask = jnp.tril(jnp.ones((B_CHUNK, B_CHUNK), dtype=bool))
    att = jnp.where(mask, jnp.einsum("bnchd,bnshd->bnhcs", qf, kf), 0.0)
    o = jnp.einsum("bnhcs,bnshv->bnchv", att, vf) + jnp.einsum("bnchd,bnhdv->bnchv", qf, s_prev)
    return o.reshape(bsz, t, h, v.shape[-1]), s_final


def gla_mixer(u_x, u_c, w_gate, b_gate, norm_g, need_ctx):
    def prep(u):
        q, k, v, r, glr = u
        bsz, t, _ = q.shape
        q = q.reshape(bsz, t, B_HEADS, B_DK) * (B_DK ** -0.5)
        k = k.reshape(bsz, t, B_HEADS, B_DK)
        v = v.reshape(bsz, t, B_HEADS, B_DV)
        pre = jnp.einsum("btzr,zrd->btzd", glr.reshape(bsz, t, 2, B_RANK).astype(jnp.float32), w_gate.astype(jnp.float32)) + b_gate.astype(jnp.float32)
        log_a = (jax.nn.log_sigmoid(pre) / B_TAU).reshape(bsz, t, 2, B_HEADS, B_DK)
        return q, k, v, r, log_a

    def out_gate(o_f, o_b, r):
        bsz, t = r.shape[:2]
        o = rms_norm(o_f + flip_seq(o_b), norm_g).astype(r.dtype)
        return (o * jax.nn.silu(r.reshape(bsz, t, B_HEADS, B_DV))).reshape(bsz, t, B_WIDTH)

    qx, kx, vx, rx, ax = prep(u_x)
    qc, kc, vc, rc, ac = prep(u_c)
    s0 = jnp.zeros((qx.shape[0], B_HEADS, B_DK, B_DV), jnp.float32)
    oc_f, sc_f = gla_chunked(qc, kc, vc, ac[:, :, 0], s0, need_ctx)
    ox_f, _ = gla_chunked(qx, kx, vx, ax[:, :, 0], sc_f, True)
    oc_b, sc_b = gla_chunked(flip_seq(qc), flip_seq(kc), flip_seq(vc), flip_seq(ac[:, :, 1]), s0, need_ctx)
    ox_b, _ = gla_chunked(flip_seq(qx), flip_seq(kx), flip_seq(vx), flip_seq(ax[:, :, 1]), sc_b, True)
    y_x = out_gate(ox_f, ox_b, rx)
    y_c = out_gate(oc_f, oc_b, rc) if need_ctx else None
    return y_x, y_c


def ssd_chunked(x, dt, a_neg, bm, cm, h0, need_out):
    bsz, t = x.shape[:2]
    n = t // C_CHUNK
    ch = lambda a: a.reshape(bsz, n, C_CHUNK, *a.shape[2:])
    x, dt, bm, cm = ch(x), ch(dt), ch(bm), ch(cm)
    xf, bf, cf = x.astype(jnp.float32), bm.astype(jnp.float32), cm.astype(jnp.float32)
    cum = jnp.cumsum(dt * a_neg, axis=2)
    cum_last = cum[:, :, -1]
    u = jnp.einsum("bksgn,bksgh,bksghp->bkghpn", bf, dt * jnp.exp(cum_last[:, :, None] - cum), xf)
    decay = jnp.exp(cum_last)

    def step(s, inp):
        d, uu = inp
        return d[..., None, None] * s + uu, s

    h_final, s_prev = lax.scan(step, h0, (jnp.moveaxis(decay, 1, 0), jnp.moveaxis(u, 1, 0)))
    if not need_out:
        return None, h_final
    s_prev = jnp.moveaxis(s_prev, 0, 1)
    cum_h = jnp.moveaxis(cum, 2, -1)
    mask = jnp.tril(jnp.ones((C_CHUNK, C_CHUNK), dtype=bool))
    seg = jnp.where(mask, cum_h[..., :, None] - cum_h[..., None, :], 0.0)
    lmat = jnp.where(mask, jnp.exp(seg), 0.0)
    cb = jnp.einsum("bkcgn,bksgn->bkgcs", cf, bf)
    w = cb[:, :, :, None] * lmat * jnp.moveaxis(dt, 2, -1)[..., None, :]
    y = jnp.einsum("bkghcs,bksghp->bkcghp", w, xf)
    y = y + jnp.einsum("bkcgn,bkcgh,bkghpn->bkcghp", cf, jnp.exp(cum), s_prev)
    return y.reshape(bsz, t, *y.shape[3:]), h_final


def ssd_mixer(u_x, u_c, conv_w, conv_b, a_log, dt_bias, d_skip, norm_g, need_ctx):
    a_neg = -jnp.exp(a_log.astype(jnp.float32)).reshape(2, C_GROUPS, C_HEADS_PER_GROUP)
    dtb = dt_bias.astype(jnp.float32).reshape(2, C_GROUPS, C_HEADS_PER_GROUP)

    def prep(u):
        xbc, z, dt_raw = u
        bsz, t, _ = xbc.shape
        xbc = jax.nn.silu(centred_conv(xbc, conv_w, conv_b))
        xs, bm, cm = jnp.split(xbc, [C_INNER, C_INNER + C_GROUPS * C_STATE], axis=-1)
        xs = xs.reshape(bsz, t, C_GROUPS, C_HEADS_PER_GROUP, C_HEAD_DIM)
        bm = bm.reshape(bsz, t, C_GROUPS, C_STATE)
        cm = cm.reshape(bsz, t, C_GROUPS, C_STATE)
        dt = jax.nn.softplus(dt_raw.astype(jnp.float32).reshape(bsz, t, 2, C_GROUPS, C_HEADS_PER_GROUP) + dtb)
        return xs, bm, cm, dt, z

    def out_gate(xs, y_f, y_b, z):
        bsz, t = z.shape[:2]
        y = y_f + flip_seq(y_b) + d_skip.astype(jnp.float32).reshape(C_GROUPS, C_HEADS_PER_GROUP, 1) * xs.astype(jnp.float32)
        y = y.reshape(bsz, t, C_INNER) * jax.nn.silu(z.astype(jnp.float32))
        y = rms_norm(y.reshape(bsz, t, C_GROUPS, C_INNER // C_GROUPS), norm_g.reshape(C_GROUPS, C_INNER // C_GROUPS))
        return y.reshape(bsz, t, C_INNER).astype(z.dtype)

    xx, bx, cx, dtx, zx = prep(u_x)
    xc, bc, cc, dtc, zc = prep(u_c)
    h0 = jnp.zeros((xx.shape[0], C_GROUPS, C_HEADS_PER_GROUP, C_HEAD_DIM, C_STATE), jnp.float32)
    yc_f, hc_f = ssd_chunked(xc, dtc[:, :, 0], a_neg[0], bc, cc, h0, need_ctx)
    yx_f, _ = ssd_chunked(xx, dtx[:, :, 0], a_neg[0], bx, cx, hc_f, True)
    yc_b, hc_b = ssd_chunked(flip_seq(xc), flip_seq(dtc[:, :, 1]), a_neg[1], flip_seq(bc), flip_seq(cc), h0, need_ctx)
    yx_b, _ = ssd_chunked(flip_seq(xx), flip_seq(dtx[:, :, 1]), a_neg[1], flip_seq(bx), flip_seq(cx), hc_b, True)
    y_x = out_gate(xx, yx_f, yx_b, zx)
    y_c = out_gate(xc, yc_f, yc_b, zc) if need_ctx else None
    return y_x, y_c


def merge_branches(o_a, o_b, o_s, gate_pre, w_br, w_out):
    bsz, t, _ = o_a.shape
    g = jax.nn.sigmoid(gate_pre.astype(jnp.float32)).astype(o_a.dtype).reshape(bsz, t, N_BRANCH, D_MODEL)
    wa, wb, ws = jnp.split(w_br, [A_WIDTH, A_WIDTH + B_WIDTH], axis=0)
    m = g[:, :, 0] * (o_a @ wa) + g[:, :, 1] * (o_b @ wb) + g[:, :, 2] * (o_s @ ws)
    return m @ w_out


def hybrid_mixer(hx, hc, p, rope, need_ctx):
    pts = split_points()
    ux = jnp.split(hx @ p["w_in"], pts, axis=-1)
    uc = jnp.split(hc @ p["w_in"], pts, axis=-1)
    ya_x, ya_c = attention_mixer(ux[0], ux[1], ux[2], uc[0], uc[1], uc[2], p["qk_g"], rope, need_ctx)
    yb_x, yb_c = gla_mixer(ux[3:8], uc[3:8], p["gla_w_gate"], p["gla_b_gate"], p["gla_norm_g"], need_ctx)
    ys_x, ys_c = ssd_mixer(ux[8:11], uc[8:11], p["ssd_conv_w"], p["ssd_conv_b"], p["ssd_a_log"], p["ssd_dt_bias"], p["ssd_d"], p["ssd_norm_g"], need_ctx)
    out_x = merge_branches(ya_x, yb_x, ys_x, ux[11], p["w_br"], p["w_out"])
    out_c = merge_branches(ya_c, yb_c, ys_c, uc[11], p["w_br"], p["w_out"]) if need_ctx else None
    return out_x, out_c


def trunk_layer(x, ctx, mx, mc, p, rope, need_ctx):
    g = p["norm_g"]
    x = x + 0.5 * mx[:, 2] * swiglu(modulate(x, g[0], mx[:, 0], mx[:, 1]), p["ffn_w_in"][0], p["ffn_w_out"][0])
    ctx = ctx + 0.5 * mc[:, 2] * swiglu(modulate(ctx, g[0], mc[:, 0], mc[:, 1]), p["ffn_w_in"][0], p["ffn_w_out"][0])
    y_x, y_c = hybrid_mixer(modulate(x, g[1], mx[:, 3], mx[:, 4]), modulate(ctx, g[1], mc[:, 3], mc[:, 4]), p, rope, need_ctx)
    x = x + mx[:, 5] * y_x
    x = x + 0.5 * mx[:, 8] * swiglu(modulate(x, g[2], mx[:, 6], mx[:, 7]), p["ffn_w_in"][1], p["ffn_w_out"][1])
    if need_ctx:
        ctx = ctx + mc[:, 5] * y_c
        ctx = ctx + 0.5 * mc[:, 8] * swiglu(modulate(ctx, g[2], mc[:, 6], mc[:, 7]), p["ffn_w_in"][1], p["ffn_w_out"][1])
    return x, ctx


def setup_inputs(seed: int = 0) -> dict:
    key = jax.random.key(seed)
    ks = jax.random.split(key, 22)
    f32 = jnp.float32

    def nrm(k, shape, s):
        return jax.random.normal(k, shape, f32) * s

    x = nrm(ks[0], (BATCH, SEQ, D_MODEL), 1.0)
    c = nrm(ks[1], (BATCH, D_MODEL), 1.0)
    ctx = nrm(ks[2], (BATCH, CTX_LEN, D_MODEL), 1.0)
    c_ctx = nrm(ks[3], (D_MODEL,), 1.0)
    w_ada = nrm(ks[4], (DEPTH, D_MODEL, N_MOD * D_MODEL), 0.5 * D_MODEL ** -0.5)
    b_ada = nrm(ks[5], (DEPTH, N_MOD * D_MODEL), 0.01)
    norm_g = 1.0 + nrm(ks[6], (DEPTH, 3, D_MODEL), 0.05)
    ffn_w_in = nrm(ks[7], (DEPTH, 2, D_MODEL, 2 * D_FF), D_MODEL ** -0.5)
    ffn_w_out = nrm(ks[8], (DEPTH, 2, D_FF, D_MODEL), D_FF ** -0.5)
    w_in = nrm(ks[9], (DEPTH, D_MODEL, IN_COLS), D_MODEL ** -0.5)
    qk_g = 1.0 + nrm(ks[10], (DEPTH, 2, A_HEAD_DIM), 0.05)
    gla_w_gate = nrm(ks[11], (DEPTH, 2, B_RANK, B_QK_WIDTH), B_RANK ** -0.5)
    gla_b_gate = nrm(ks[12], (DEPTH, 2, B_QK_WIDTH), 0.1)
    gla_norm_g = 1.0 + nrm(ks[13], (DEPTH, B_DV), 0.05)
    ssd_conv_w = nrm(ks[14], (DEPTH, C_CONV, C_XBC), C_CONV ** -0.5)
    ssd_conv_b = nrm(ks[15], (DEPTH, C_XBC), 0.01)
    ssd_a_log = jnp.log(jax.random.uniform(ks[16], (DEPTH, 2, C_HEADS), f32, 1.0, 16.0))
    dt0 = jnp.exp(jax.random.uniform(ks[17], (DEPTH, 2, C_HEADS), f32, math.log(1e-3), math.log(1e-1)))
    ssd_dt_bias = dt0 + jnp.log(-jnp.expm1(-dt0))
    ssd_d = 1.0 + nrm(ks[18], (DEPTH, C_HEADS), 0.1)
    ssd_norm_g = 1.0 + nrm(ks[19], (DEPTH, C_INNER), 0.05)
    w_br = nrm(ks[20], (DEPTH, MIX_WIDTH, D_MODEL), A_WIDTH ** -0.5)
    w_out = nrm(ks[21], (DEPTH, D_MODEL, D_MODEL), D_MODEL ** -0.5)
    return {"x": x, "c": c, "ctx": ctx, "c_ctx": c_ctx, "w_ada": w_ada, "b_ada": b_ada, "norm_g": norm_g, "ffn_w_in": ffn_w_in, "ffn_w_out": ffn_w_out, "w_in": w_in, "qk_g": qk_g, "gla_w_gate": gla_w_gate, "gla_b_gate": gla_b_gate, "gla_norm_g": gla_norm_g, "ssd_conv_w": ssd_conv_w, "ssd_conv_b": ssd_conv_b, "ssd_a_log": ssd_a_log, "ssd_dt_bias": ssd_dt_bias, "ssd_d": ssd_d, "ssd_norm_g": ssd_norm_g, "w_br": w_br, "w_out": w_out}


def reference(x, c, ctx, c_ctx, w_ada, b_ada, norm_g, ffn_w_in, ffn_w_out, w_in, qk_g, gla_w_gate, gla_b_gate, gla_norm_g, ssd_conv_w, ssd_conv_b, ssd_a_log, ssd_dt_bias, ssd_d, ssd_norm_g, w_br, w_out):
    n_tok = x.shape[1]
    rows_n = n_tok // GRID_W
    rope = rope_tables(rows_n)
    sc = jax.nn.silu(c)
    sc_ctx = jax.nn.silu(c_ctx)
    for l in range(DEPTH):
        p = {"norm_g": norm_g[l], "ffn_w_in": ffn_w_in[l], "ffn_w_out": ffn_w_out[l], "w_in": w_in[l], "qk_g": qk_g[l], "gla_w_gate": gla_w_gate[l], "gla_b_gate": gla_b_gate[l], "gla_norm_g": gla_norm_g[l], "ssd_conv_w": ssd_conv_w[l], "ssd_conv_b": ssd_conv_b[l], "ssd_a_log": ssd_a_log[l], "ssd_dt_bias": ssd_dt_bias[l], "ssd_d": ssd_d[l], "ssd_norm_g": ssd_norm_g[l], "w_br": w_br[l], "w_out": w_out[l]}
        mx = (sc @ w_ada[l] + b_ada[l]).reshape(sc.shape[0], N_MOD, 1, D_MODEL)
        mc = (sc_ctx @ w_ada[l] + b_ada[l]).reshape(1, N_MOD, 1, D_MODEL)
        x, ctx = trunk_layer(x, ctx, mx, mc, p, rope, l < DEPTH - 1)
    return x
```

```python
import functools
import math

import jax
import jax.numpy as jnp
from jax import lax
from jax.experimental import pallas as pl
from jax.experimental.pallas import tpu as pltpu

F32 = jnp.float32
BF16 = jnp.bfloat16

D_MODEL = 2048
DEPTH = 4
GRID_W = 64
EPS = 1e-6
N_MOD = 9
D_FF = 5632

A_HEADS = 8
A_KV_HEADS = 2
A_HEAD_DIM = 128
ROPE_THETA = 10000.0
A_WIDTH = A_HEADS * A_HEAD_DIM
A_KV_WIDTH = A_KV_HEADS * A_HEAD_DIM

B_HEADS = 4
B_DK = 128
B_DV = 256
B_RANK = 16
B_TAU = 16.0
B_CHUNK = 64
B_QK_WIDTH = B_HEADS * B_DK
B_WIDTH = B_HEADS * B_DV

C_HEADS = 16
C_HEAD_DIM = 64
C_GROUPS = 2
C_STATE = 128
C_CONV = 5
C_CHUNK = 128
C_INNER = C_HEADS * C_HEAD_DIM
C_XBC = C_INNER + 2 * C_GROUPS * C_STATE
N_BRANCH = 3

P_AQ = 0
P_BV = 1024
P_BR = 2048
P_AKV = 3072
P_BQ = 3584
P_BK = 4096
P_XBC = 4608
P_Z = 6144
P_GATE = 7168
P_SMALL = 13312
P_COLS = 13824
SMALL_DT = 32

ROW_TILE = 256
VMEM_LIMIT = 56 * 1024 * 1024


def _cparams(n_axes):
    return pltpu.CompilerParams(dimension_semantics=("arbitrary",) * n_axes, vmem_limit_bytes=VMEM_LIMIT)


def _pick_tile(total, candidates):
    for c in candidates:
        if total % c == 0:
            return c
    raise ValueError(f"no tile for {total}")


def _silu(x):
    return x * (1.0 / (1.0 + jnp.exp(-x)))


def _sigmoid(x):
    return 1.0 / (1.0 + jnp.exp(-x))


def _split3(x):
    hi = x.astype(BF16)
    r1 = x - hi.astype(F32)
    mid = r1.astype(BF16)
    lo = (r1 - mid.astype(F32)).astype(BF16)
    return hi, mid, lo


def _dot(a, b):
    return jnp.dot(a, b, preferred_element_type=F32)


def _dot_exact_lhs(sel, x):
    hi, mid, lo = _split3(x)
    return _dot(sel, hi) + _dot(sel, mid) + _dot(sel, lo)


def _dot_exact_rhs(x, sel):
    hi, mid, lo = _split3(x)
    return _dot(hi, sel) + _dot(mid, sel) + _dot(lo, sel)


def _ada_kernel(c_ref, w_ref, b_ref, o_ref):
    a = _silu(c_ref[...]).astype(BF16)
    o_ref[...] = _dot(a, w_ref[...].astype(BF16)) + b_ref[...]


def _ada_call(cc, w_ada, b_ada):
    depth, d, n = w_ada.shape
    tn = 1024
    return pl.pallas_call(
        _ada_kernel,
        grid=(depth, n // tn),
        in_specs=[
            pl.BlockSpec((16, d), lambda l, j: (0, 0)),
            pl.BlockSpec((None, d, tn), lambda l, j: (l, 0, j)),
            pl.BlockSpec((None, 1, tn), lambda l, j: (l, 0, j)),
        ],
        out_specs=pl.BlockSpec((None, 16, tn), lambda l, j: (l, 0, j)),
        out_shape=jax.ShapeDtypeStruct((depth, 16, n), F32),
        compiler_params=_cparams(2),
        name="ada_mod",
    )(cc, w_ada, b_ada.reshape(depth, 1, n))


def _norm_mod_kernel(x_ref, g_ref, sh_ref, sc_ref, o_ref):
    x = x_ref[...]
    y = x * lax.rsqrt(jnp.mean(x * x, axis=-1, keepdims=True) + EPS) * g_ref[...]
    o_ref[...] = (y * (1.0 + sc_ref[...]) + sh_ref[...]).astype(o_ref.dtype)


def _norm_mod_call(xs, norm_g, mods, l, gi, shift_i, scale_i):
    t, d = xs.shape
    tm = ROW_TILE
    row = lambda i: jnp.minimum(i, 1)
    return pl.pallas_call(
        _norm_mod_kernel,
        grid=(t // tm,),
        in_specs=[
            pl.BlockSpec((tm, d), lambda i: (i, 0)),
            pl.BlockSpec((None, None, 1, d), lambda i: (l, gi, 0, 0)),
            pl.BlockSpec((None, None, None, 1, d), lambda i: (l, row(i), shift_i, 0, 0)),
            pl.BlockSpec((None, None, None, 1, d), lambda i: (l, row(i), scale_i, 0, 0)),
        ],
        out_specs=pl.BlockSpec((tm, d), lambda i: (i, 0)),
        out_shape=jax.ShapeDtypeStruct((t, d), BF16),
        compiler_params=_cparams(1),
        name="norm_mod",
    )(xs, norm_g.reshape(norm_g.shape[0], 3, 1, d), mods, mods)


def _ffn_in_kernel(a_ref, wa_ref, wu_ref, o_ref, wa_bf, wu_bf):
    @pl.when(pl.program_id(1) == 0)
    def _():
        wa_bf[...] = wa_ref[...].astype(BF16)
        wu_bf[...] = wu_ref[...].astype(BF16)

    a = a_ref[...]
    ga = _dot(a, wa_bf[...])
    gu = _dot(a, wu_bf[...])
    o_ref[...] = (_silu(ga) * gu).astype(o_ref.dtype)


def _ffn_in_call(h, ffn_w_in, l, f):
    t, d = h.shape
    tn = 512
    tm = _pick_tile(t, (768, 512, 256))
    nj = D_FF // tn
    return pl.pallas_call(
        _ffn_in_kernel,
        grid=(nj, t // tm),
        in_specs=[
            pl.BlockSpec((tm, d), lambda j, i: (i, 0)),
            pl.BlockSpec((None, None, d, tn), lambda j, i: (l, f, 0, j)),
            pl.BlockSpec((None, None, d, tn), lambda j, i: (l, f, 0, j + nj)),
        ],
        out_specs=pl.BlockSpec((tm, tn), lambda j, i: (i, j)),
        out_shape=jax.ShapeDtypeStruct((t, D_FF), BF16),
        scratch_shapes=[pltpu.VMEM((d, tn), BF16), pltpu.VMEM((d, tn), BF16)],
        compiler_params=_cparams(2),
        name="ffn_in",
    )(h, ffn_w_in, ffn_w_in)


def _mm_res_kernel(a_ref, w_ref, res_ref, gate_ref, o_ref, w_bf, *, factor, tm, n_ctx):
    @pl.when(pl.program_id(1) == 0)
    def _():
        w_bf[...] = w_ref[...].astype(BF16)

    acc = _dot(a_ref[...], w_bf[...])
    rows = pl.program_id(1) * tm + lax.broadcasted_iota(jnp.int32, (tm, 1), 0)
    gate = jnp.where(rows < n_ctx, gate_ref[0], gate_ref[1])
    if factor != 1.0:
        gate = factor * gate
    o_ref[...] = res_ref[...] + gate * acc


def _mm_res_call(a, w_spec_shape, w_index, w, xs, mods, l, gate_i, factor, n_ctx, tm_cands, tn, name):
    t, k = a.shape
    d = xs.shape[1]
    tm = _pick_tile(t, tm_cands)
    return pl.pallas_call(
        functools.partial(_mm_res_kernel, factor=factor, tm=tm, n_ctx=n_ctx),
        grid=(d // tn, t // tm),
        in_specs=[
            pl.BlockSpec((tm, k), lambda j, i: (i, 0)),
            pl.BlockSpec(w_spec_shape + (k, tn), lambda j, i: w_index + (0, j)),
            pl.BlockSpec((tm, tn), lambda j, i: (i, j)),
            pl.BlockSpec((None, 2, None, 1, tn), lambda j, i: (l, 0, gate_i, 0, j)),
        ],
        out_specs=pl.BlockSpec((tm, tn), lambda j, i: (i, j)),
        out_shape=jax.ShapeDtypeStruct((t, d), F32),
        scratch_shapes=[pltpu.VMEM((k, tn), BF16)],
        compiler_params=_cparams(2),
        name=name,
    )(a, w, xs, mods)


def _proj_kernel(a_ref, w_ref, o_ref):
    o_ref[...] = _dot(a_ref[...], w_ref[...])


def _proj_call(h, w_in_p, l):
    t, d = h.shape
    tn = 1536
    tm = _pick_tile(t, (768, 512, 256))
    return pl.pallas_call(
        _proj_kernel,
        grid=(P_COLS // tn, t // tm),
        in_specs=[
            pl.BlockSpec((tm, d), lambda j, i: (i, 0)),
            pl.BlockSpec((None, d, tn), lambda j, i: (l, 0, j)),
        ],
        out_specs=pl.BlockSpec((tm, tn), lambda j, i: (i, j)),
        out_shape=jax.ShapeDtypeStruct((t, P_COLS), F32),
        compiler_params=_cparams(2),
        name="in_proj",
    )(h, w_in_p)


def _qk_prep_kernel(q_in, kv_in, cos_ref, sin_ref, g_ref, q_ref, k_ref, v_ref):
    tm = q_in.shape[0]
    cs = cos_ref[...]
    sn = sin_ref[...]
    lane = lax.broadcasted_iota(jnp.int32, (tm, A_HEAD_DIM), 1)
    first = (lane % 64) < 32

    def norm_rope(x, g):
        y = x * lax.rsqrt(jnp.mean(x * x, axis=-1, keepdims=True) + EPS) * g
        partner = jnp.where(first, pltpu.roll(y, 96, 1), pltpu.roll(y, 32, 1))
        return y * cs + partner * sn

    for h in range(A_HEADS):
        q_ref[h] = norm_rope(q_in[:, h * 128:(h + 1) * 128], g_ref[0:1, :]).astype(BF16)
    for h in range(A_KV_HEADS):
        k_ref[h] = norm_rope(kv_in[:, h * 128:(h + 1) * 128], g_ref[1:2, :]).astype(BF16)
        v_ref[h] = kv_in[:, A_KV_WIDTH + h * 128:A_KV_WIDTH + (h + 1) * 128].astype(BF16)


def _qk_prep_call(u, rope_c, rope_s, qk_g, l):
    t = u.shape[0]
    tm = ROW_TILE
    return pl.pallas_call(
        _qk_prep_kernel,
        grid=(t // tm,),
        in_specs=[
            pl.BlockSpec((tm, A_WIDTH), lambda i: (i, P_AQ // A_WIDTH)),
            pl.BlockSpec((tm, 2 * A_KV_WIDTH), lambda i: (i, P_AKV // (2 * A_KV_WIDTH))),
            pl.BlockSpec((tm, A_HEAD_DIM), lambda i: (i, 0)),
            pl.BlockSpec((tm, A_HEAD_DIM), lambda i: (i, 0)),
            pl.BlockSpec((None, 2, A_HEAD_DIM), lambda i: (l, 0, 0)),
        ],
        out_specs=[
            pl.BlockSpec((A_HEADS, tm, A_HEAD_DIM), lambda i: (0, i, 0)),
            pl.BlockSpec((A_KV_HEADS, tm, A_HEAD_DIM), lambda i: (0, i, 0)),
            pl.BlockSpec((A_KV_HEADS, tm, A_HEAD_DIM), lambda i: (0, i, 0)),
        ],
        out_shape=[
            jax.ShapeDtypeStruct((A_HEADS, t, A_HEAD_DIM), BF16),
            jax.ShapeDtypeStruct((A_KV_HEADS, t, A_HEAD_DIM), BF16),
            jax.ShapeDtypeStruct((A_KV_HEADS, t, A_HEAD_DIM), BF16),
        ],
        compiler_params=_cparams(1),
        name="qk_prep",
    )(u, u, rope_c, rope_s, qk_g)


def _attn_kernel(q_ref, k_ref, v_ref, o_ref, m_sc, l_sc, acc_sc, *, tq, n_ctx, t_all, ck):
    grp = A_HEADS // A_KV_HEADS
    rows = grp * tq
    q = q_ref[...].reshape(rows, A_HEAD_DIM)
    c2 = (A_HEAD_DIM ** -0.5) * math.log2(math.e)

    def init():
        m_sc[...] = jnp.full((rows, 1), -jnp.inf, F32)
        l_sc[...] = jnp.zeros((rows, 1), F32)
        acc_sc[...] = jnp.zeros((rows, A_HEAD_DIM), F32)

    def step(k, v):
        s = lax.dot_general(q, k, (((1,), (1,)), ((), ())), preferred_element_type=F32)
        m_prev = m_sc[...]
        m_new = jnp.maximum(m_prev, jnp.max(s, axis=-1, keepdims=True))
        alpha = jnp.exp2((m_prev - m_new) * c2)
        p = jnp.exp2((s - m_new) * c2)
        l_sc[...] = alpha * l_sc[...] + jnp.sum(p, axis=-1, keepdims=True)
        acc_sc[...] = alpha * acc_sc[...] + _dot(p.astype(BF16), v)
        m_sc[...] = m_new

    def finish():
        out = acc_sc[...] / l_sc[...]
        for h in range(grp):
            o_ref[:, h * A_HEAD_DIM:(h + 1) * A_HEAD_DIM] = out[h * tq:(h + 1) * tq].astype(o_ref.dtype)

    i = pl.program_id(1)

    @pl.when(i == 0)
    def _():
        init()
        step(k_ref[0:n_ctx, :], v_ref[0:n_ctx, :])
        finish()

    @pl.when(i > 0)
    def _():
        init()

        def body(c, carry):
            off = pl.multiple_of(c * ck, ck)
            step(k_ref[pl.ds(off, ck), :], v_ref[pl.ds(off, ck), :])
            return carry

        lax.fori_loop(0, t_all // ck, body, 0)
        finish()


def _attn_call(q, k, v, n_ctx):
    _, t, dh = q.shape
    tq = ROW_TILE
    grp = A_HEADS // A_KV_HEADS
    ck = _pick_tile(t, (768, 512, 256))
    return pl.pallas_call(
        functools.partial(_attn_kernel, tq=tq, n_ctx=n_ctx, t_all=t, ck=ck),
        grid=(A_KV_HEADS, t // tq),
        in_specs=[
            pl.BlockSpec((grp, tq, dh), lambda g, i: (g, i, 0)),
            pl.BlockSpec((None, t, dh), lambda g, i: (g, 0, 0)),
            pl.BlockSpec((None, t, dh), lambda g, i: (g, 0, 0)),
        ],
        out_specs=pl.BlockSpec((tq, grp * dh), lambda g, i: (i, g)),
        out_shape=jax.ShapeDtypeStruct((t, A_WIDTH), BF16),
        scratch_shapes=[
            pltpu.VMEM((grp * tq, 1), F32),
            pltpu.VMEM((grp * tq, 1), F32),
            pltpu.VMEM((grp * tq, dh), F32),
        ],
        compiler_params=_cparams(2),
        name="attn",
    )(q, k, v)


def _bwd_block(i, nb):
    return jnp.where(i == 0, 0, nb - i)


def _gla_dir(q_ref, k_ref, v_ref, sm_ref, wg_ref, bg_ref, o_ref, st_ref, reverse):
    tm = q_ref.shape[0]
    nch = tm // B_CHUNK
    r = lax.broadcasted_iota(jnp.int32, (tm, tm), 0)
    c = lax.broadcasted_iota(jnp.int32, (tm, tm), 1)
    same = (r // B_CHUNK) == (c // B_CHUNK)
    tri = same & ((c >= r) if reverse else (c <= r))
    tri_bf = tri.astype(BF16)
    same_bf = same.astype(BF16)

    pre = jnp.dot(sm_ref[...], wg_ref[...], precision=lax.Precision.HIGHEST, preferred_element_type=F32) + bg_ref[...]
    la = (jnp.minimum(pre, 0.0) - jnp.log1p(jnp.exp(-jnp.abs(pre)))) / B_TAU
    hi, mid, lo = _split3(la)
    b = _dot(tri_bf, hi) + _dot(tri_bf, mid) + _dot(tri_bf, lo)
    tot = _dot(same_bf, hi) + _dot(same_bf, mid) + _dot(same_bf, lo)
    eb = jnp.exp(b)
    qf = (q_ref[...] * (B_DK ** -0.5)) * eb
    kf = k_ref[...] * jnp.exp(-b)
    k_end = k_ref[...] * jnp.exp(tot - b)
    dec = jnp.exp(tot)
    v = v_ref[...].astype(BF16)
    qf_bf = qf.astype(BF16)
    kf_bf = kf.astype(BF16)
    ke_bf = k_end.astype(BF16)

    order = range(nch - 1, -1, -1) if reverse else range(nch)
    for h in range(B_HEADS):
        ks = slice(h * B_DK, (h + 1) * B_DK)
        vs = slice(h * B_DV, (h + 1) * B_DV)
        att = lax.dot_general(qf_bf[:, ks], kf_bf[:, ks], (((1,), (1,)), ((), ())), preferred_element_type=F32)
        att = jnp.where(tri, att, 0.0).astype(BF16)
        o_intra = _dot(att, v[:, vs])
        for j in order:
            rs = slice(j * B_CHUNK, (j + 1) * B_CHUNK)
            st = st_ref[h]
            o_inter = lax.dot_general(qf_bf[rs, ks], st.astype(BF16), (((1,), (1,)), ((), ())),
                                      preferred_element_type=F32)
            o_ref[rs, vs] = o_intra[rs] + o_inter
            ut = lax.dot_general(v[rs, vs], ke_bf[rs, ks], (((0,), (0,)), ((), ())), preferred_element_type=F32)
            st_ref[h] = dec[j * B_CHUNK:j * B_CHUNK + 1, ks] * st + ut


def _gla_kernel(qf_ref, kf_ref, vf_ref, sf_ref, qb_ref, kb_ref, vb_ref, sb_ref, wg_ref, bg_ref,
                of_ref, ob_ref, stf_ref, stb_ref):
    @pl.when(pl.program_id(0) == 0)
    def _():
        stf_ref[...] = jnp.zeros_like(stf_ref)
        stb_ref[...] = jnp.zeros_like(stb_ref)

    _gla_dir(qf_ref, kf_ref, vf_ref, sf_ref, wg_ref.at[0], bg_ref.at[0], of_ref, stf_ref, False)
    _gla_dir(qb_ref, kb_ref, vb_ref, sb_ref, wg_ref.at[1], bg_ref.at[1], ob_ref, stb_ref, True)


def _gla_call(u, wg_p, bg):
    t = u.shape[0]
    tm = ROW_TILE
    nb = t // tm
    fwd = lambda i: i
    bwd = lambda i: _bwd_block(i, nb)

    def views(blk):
        return [
            pl.BlockSpec((tm, B_QK_WIDTH), lambda i: (blk(i), P_BQ // B_QK_WIDTH)),
            pl.BlockSpec((tm, B_QK_WIDTH), lambda i: (blk(i), P_BK // B_QK_WIDTH)),
            pl.BlockSpec((tm, B_WIDTH), lambda i: (blk(i), P_BV // B_WIDTH)),
            pl.BlockSpec((tm, 128), lambda i: (blk(i), P_SMALL // 128)),
        ]

    return pl.pallas_call(
        _gla_kernel,
        grid=(nb,),
        in_specs=views(fwd) + views(bwd) + [
            pl.BlockSpec((2, 128, B_QK_WIDTH), lambda i: (0, 0, 0)),
            pl.BlockSpec((2, 1, B_QK_WIDTH), lambda i: (0, 0, 0)),
        ],
        out_specs=[
            pl.BlockSpec((tm, B_WIDTH), lambda i: (fwd(i), 0)),
            pl.BlockSpec((tm, B_WIDTH), lambda i: (bwd(i), 0)),
        ],
        out_shape=[jax.ShapeDtypeStruct((t, B_WIDTH), F32), jax.ShapeDtypeStruct((t, B_WIDTH), F32)],
        scratch_shapes=[pltpu.VMEM((B_HEADS, B_DV, B_DK), F32), pltpu.VMEM((B_HEADS, B_DV, B_DK), F32)],
        compiler_params=_cparams(1),
        name="gla_scan",
    )(u, u, u, u, u, u, u, u, wg_p, bg)


def _gla_out_kernel(of_ref, ob_ref, r_ref, g_ref, y_ref):
    for h in range(B_HEADS):
        vs = slice(h * B_DV, (h + 1) * B_DV)
        o = of_ref[:, vs] + ob_ref[:, vs]
        y = o * lax.rsqrt(jnp.mean(o * o, axis=-1, keepdims=True) + EPS) * g_ref[...]
        y_ref[:, vs] = (y * _silu(r_ref[:, vs])).astype(y_ref.dtype)


def _gla_out_call(o_f, o_b, u, gla_norm_g, l):
    t = u.shape[0]
    tm = ROW_TILE
    return pl.pallas_call(
        _gla_out_kernel,
        grid=(t // tm,),
        in_specs=[
            pl.BlockSpec((tm, B_WIDTH), lambda i: (i, 0)),
            pl.BlockSpec((tm, B_WIDTH), lambda i: (i, 0)),
            pl.BlockSpec((tm, B_WIDTH), lambda i: (i, P_BR // B_WIDTH)),
            pl.BlockSpec((None, 1, B_DV), lambda i: (l, 0, 0)),
        ],
        out_specs=pl.BlockSpec((tm, B_WIDTH), lambda i: (i, 0)),
        out_shape=jax.ShapeDtypeStruct((t, B_WIDTH), BF16),
        compiler_params=_cparams(1),
        name="gla_out",
    )(o_f, o_b, u, gla_norm_g.reshape(gla_norm_g.shape[0], 1, B_DV))


def _ssd_conv_kernel(prev_ref, cur_ref, next_ref, w_ref, b_ref, o_ref, *, nb):
    i = pl.program_id(0)
    tm = cur_ref.shape[0]
    prev_ok = (i >= 2).astype(F32)
    next_ok = ((i >= 1) & (i < nb - 1)).astype(F32)
    ext = jnp.concatenate([prev_ref[...] * prev_ok, cur_ref[...], next_ref[...] * next_ok], axis=0)
    n_ext = tm + 16
    acc = jnp.zeros((tm, C_XBC), F32) + b_ref[...]
    for j in range(C_CONV):
        shift = (C_CONV // 2 - j) % n_ext
        sh = ext if shift == 0 else pltpu.roll(ext, shift, 0)
        acc = acc + sh[8:8 + tm, :] * w_ref[j:j + 1, :]
    o_ref[...] = _silu(acc)


def _ssd_conv_call(u, conv_w, conv_b, l):
    t = u.shape[0]
    tm = ROW_TILE
    nb = t // tm
    r8 = tm // 8
    cb = P_XBC // C_XBC
    return pl.pallas_call(
        functools.partial(_ssd_conv_kernel, nb=nb),
        grid=(nb,),
        in_specs=[
            pl.BlockSpec((8, C_XBC), lambda i: (jnp.maximum(i * r8 - 1, 0), cb)),
            pl.BlockSpec((tm, C_XBC), lambda i: (i, cb)),
            pl.BlockSpec((8, C_XBC), lambda i: (jnp.minimum((i + 1) * r8, nb * r8 - 1), cb)),
            pl.BlockSpec((None, 8, C_XBC), lambda i: (l, 0, 0)),
            pl.BlockSpec((None, 1, C_XBC), lambda i: (l, 0, 0)),
        ],
        out_specs=pl.BlockSpec((tm, C_XBC), lambda i: (i, 0)),
        out_shape=jax.ShapeDtypeStruct((t, C_XBC), F32),
        compiler_params=_cparams(1),
        name="ssd_conv",
    )(u, u, u, conv_w, conv_b)


def _ssd_dir(xc_ref, sm_ref, dtb_ref, aneg_ref, e_ref, y_ref, st_ref, d, reverse):
    tm = xc_ref.shape[0]
    nch = tm // C_CHUNK
    cc = C_CHUNK
    r = lax.broadcasted_iota(jnp.int32, (cc, cc), 0)
    c = lax.broadcasted_iota(jnp.int32, (cc, cc), 1)
    tri = (c >= r) if reverse else (c <= r)
    tri_bf = tri.astype(BF16)
    ones_bf = jnp.ones((cc, cc), BF16)
    lane = lax.broadcasted_iota(jnp.int32, (cc, 128), 1)
    left = lane < C_HEAD_DIM
    e_bf = e_ref[...]

    order = range(nch - 1, -1, -1) if reverse else range(nch)
    for j in order:
        rs = slice(j * cc, (j + 1) * cc)
        x = xc_ref[rs, 0:C_INNER]
        dt_raw = sm_ref[rs, :] + dtb_ref[...]
        dt = jnp.maximum(dt_raw, 0.0) + jnp.log1p(jnp.exp(-jnp.abs(dt_raw)))
        a = dt * aneg_ref[...]
        hi, mid, lo = _split3(a)
        cum = _dot(tri_bf, hi) + _dot(tri_bf, mid) + _dot(tri_bf, lo)
        tot = _dot(ones_bf, hi) + _dot(ones_bf, mid) + _dot(ones_bf, lo)
        cum_t = cum.T
        dt_e = _dot_exact_rhs(dt, e_bf)
        cum_e = _dot_exact_rhs(cum, e_bf)
        tot_e = _dot_exact_rhs(tot, e_bf)
        xd = x * dt_e
        xw_bf = (xd * jnp.exp(tot_e - cum_e)).astype(BF16)
        xd_bf = xd.astype(BF16)
        ec_e = jnp.exp(cum_e)
        dec_e = jnp.exp(tot_e[0:1, :])
        for g in range(C_GROUPS):
            bm = xc_ref[rs, C_INNER + g * C_STATE:C_INNER + (g + 1) * C_STATE]
            cm = xc_ref[rs, C_INNER + (C_GROUPS + g) * C_STATE:C_INNER + (C_GROUPS + g + 1) * C_STATE]
            bm_bf = bm.astype(BF16)
            cm_bf = cm.astype(BF16)
            cb = lax.dot_general(cm_bf, bm_bf, (((1,), (1,)), ((), ())), preferred_element_type=F32)
            bt_bf = bm.T.astype(BF16)
            for pr in range(C_HEADS // C_GROUPS // 2):
                pair = g * (C_HEADS // C_GROUPS // 2) + pr
                ls = slice(pair * 128, (pair + 1) * 128)
                y_pair = None
                for half in range(2):
                    hl = SMALL_DT + 16 * d + 2 * pair + half
                    col = jnp.broadcast_to(cum[:, hl:hl + 1], (cc, cc))
                    row = jnp.broadcast_to(cum_t[hl:hl + 1, :], (cc, cc))
                    w = (cb * jnp.where(tri, jnp.exp(col - row), 0.0)).astype(BF16)
                    xm = jnp.where(left if half == 0 else ~left, xd_bf[:, ls], jnp.zeros((), BF16))
                    part = _dot(w, xm)
                    y_pair = part if y_pair is None else y_pair + part
                st = st_ref[pair]
                y_ref[rs, ls] = y_pair + _dot(cm_bf, st.astype(BF16)) * ec_e[:, ls]
                st_ref[pair] = dec_e[:, ls] * st + _dot(bt_bf, xw_bf[:, ls])


def _ssd_kernel(xf_ref, sf_ref, xb_ref, sb_ref, dtb_ref, aneg_ref, e_ref, yf_ref, yb_ref, stf_ref, stb_ref):
    @pl.when(pl.program_id(0) == 0)
    def _():
        stf_ref[...] = jnp.zeros_like(stf_ref)
        stb_ref[...] = jnp.zeros_like(stb_ref)

    _ssd_dir(xf_ref, sf_ref, dtb_ref, aneg_ref, e_ref.at[0], yf_ref, stf_ref, 0, False)
    _ssd_dir(xb_ref, sb_ref, dtb_ref, aneg_ref, e_ref.at[1], yb_ref, stb_ref, 1, True)


def _ssd_call(xc, u, dtb_p, aneg_p, e_mat):
    t = u.shape[0]
    tm = ROW_TILE
    nb = t // tm
    fwd = lambda i: i
    bwd = lambda i: _bwd_block(i, nb)

    def views(blk):
        return [
            pl.BlockSpec((tm, C_XBC), lambda i: (blk(i), 0)),
            pl.BlockSpec((tm, 128), lambda i: (blk(i), P_SMALL // 128)),
        ]

    return pl.pallas_call(
        _ssd_kernel,
        grid=(nb,),
        in_specs=views(fwd) + views(bwd) + [
            pl.BlockSpec((1, 128), lambda i: (0, 0)),
            pl.BlockSpec((1, 128), lambda i: (0, 0)),
            pl.BlockSpec((2, 128, C_INNER), lambda i: (0, 0, 0)),
        ],
        out_specs=[
            pl.BlockSpec((tm, C_INNER), lambda i: (fwd(i), 0)),
            pl.BlockSpec((tm, C_INNER), lambda i: (bwd(i), 0)),
        ],
        out_shape=[jax.ShapeDtypeStruct((t, C_INNER), F32), jax.ShapeDtypeStruct((t, C_INNER), F32)],
        scratch_shapes=[pltpu.VMEM((C_HEADS // 2, C_STATE, 128), F32), pltpu.VMEM((C_HEADS // 2, C_STATE, 128), F32)],
        compiler_params=_cparams(1),
        name="ssd_scan",
    )(xc, u, xc, u, dtb_p, aneg_p, e_mat)


def _ssd_out_kernel(yf_ref, yb_ref, xc_ref, z_ref, d_ref, g_ref, y_ref):
    gw = C_INNER // C_GROUPS
    for g in range(C_GROUPS):
        ls = slice(g * gw, (g + 1) * gw)
        y = yf_ref[:, ls] + yb_ref[:, ls] + d_ref[:, ls] * xc_ref[:, ls]
        y = y * _silu(z_ref[:, ls])
        y = y * lax.rsqrt(jnp.mean(y * y, axis=-1, keepdims=True) + EPS) * g_ref[:, ls]
        y_ref[:, ls] = y.astype(y_ref.dtype)


def _ssd_out_call(y_f, y_b, xc, u, d_e, ssd_norm_g, l):
    t = u.shape[0]
    tm = ROW_TILE
    return pl.pallas_call(
        _ssd_out_kernel,
        grid=(t // tm,),
        in_specs=[
            pl.BlockSpec((tm, C_INNER), lambda i: (i, 0)),
            pl.BlockSpec((tm, C_INNER), lambda i: (i, 0)),
            pl.BlockSpec((tm, C_INNER), lambda i: (i, 0)),
            pl.BlockSpec((tm, C_INNER), lambda i: (i, P_Z // C_INNER)),
            pl.BlockSpec((None, 1, C_INNER), lambda i: (l, 0, 0)),
            pl.BlockSpec((None, 1, C_INNER), lambda i: (l, 0, 0)),
        ],
        out_specs=pl.BlockSpec((tm, C_INNER), lambda i: (i, 0)),
        out_shape=jax.ShapeDtypeStruct((t, C_INNER), BF16),
        compiler_params=_cparams(1),
        name="ssd_out",
    )(y_f, y_b, xc, u, d_e, ssd_norm_g.reshape(ssd_norm_g.shape[0], 1, C_INNER))


def _merge_kernel(oa_ref, ob_ref, os_ref, wa_ref, wb_ref, ws_ref, ga_ref, gb_ref, gs_ref, o_ref, w_bf):
    @pl.when(pl.program_id(1) == 0)
    def _():
        w_bf[0] = wa_ref[...].astype(BF16)
        w_bf[1] = wb_ref[...].astype(BF16)
        w_bf[2] = ws_ref[...].astype(BF16)

    m = _sigmoid(ga_ref[...]) * _dot(oa_ref[...], w_bf[0])
    m = m + _sigmoid(gb_ref[...]) * _dot(ob_ref[...], w_bf[1])
    m = m + _sigmoid(gs_ref[...]) * _dot(os_ref[...], w_bf[2])
    o_ref[...] = m.astype(o_ref.dtype)


def _merge_call(o_a, o_b, o_s, u, w_br, l):
    t = u.shape[0]
    tn = 512
    tm = _pick_tile(t, (768, 512, 256))
    k = A_WIDTH
    gb = P_GATE // tn
    nd = D_MODEL // tn
    a_spec = pl.BlockSpec((tm, k), lambda j, i: (i, 0))
    return pl.pallas_call(
        _merge_kernel,
        grid=(nd, t // tm),
        in_specs=[
            a_spec, a_spec, a_spec,
            pl.BlockSpec((None, k, tn), lambda j, i: (l, 0, j)),
            pl.BlockSpec((None, k, tn), lambda j, i: (l, 1, j)),
            pl.BlockSpec((None, k, tn), lambda j, i: (l, 2, j)),
            pl.BlockSpec((tm, tn), lambda j, i: (i, gb + j)),
            pl.BlockSpec((tm, tn), lambda j, i: (i, gb + nd + j)),
            pl.BlockSpec((tm, tn), lambda j, i: (i, gb + 2 * nd + j)),
        ],
        out_specs=pl.BlockSpec((tm, tn), lambda j, i: (i, j)),
        out_shape=jax.ShapeDtypeStruct((t, D_MODEL), BF16),
        scratch_shapes=[pltpu.VMEM((N_BRANCH, k, tn), BF16)],
        compiler_params=_cparams(2),
        name="branch_merge",
    )(o_a, o_b, o_s, w_br, w_br, w_br, u, u, u)


def _repack_w_in(w_in):
    sizes = (A_WIDTH, A_KV_WIDTH, A_KV_WIDTH, B_QK_WIDTH, B_QK_WIDTH, B_WIDTH, B_WIDTH, 2 * B_RANK, C_XBC, C_INNER,
             2 * C_HEADS, N_BRANCH * D_MODEL)
    offs = [0]
    for s in sizes:
        offs.append(offs[-1] + s)
    seg = lambda n: w_in[:, :, offs[n]:offs[n + 1]]
    depth, d, _ = w_in.shape
    pad = jnp.zeros((depth, d, P_COLS - P_SMALL - 2 * B_RANK - 2 * C_HEADS), w_in.dtype)
    parts = [seg(0), seg(5), seg(6), seg(1), seg(2), seg(3), seg(4), seg(8), seg(9), seg(11), seg(7), seg(10), pad]
    return jnp.concatenate(parts, axis=-1).astype(BF16)


def _rope_tables(n_ctx, n_tok):
    rows_n = n_tok // GRID_W
    rows = jnp.repeat(jnp.arange(rows_n), GRID_W)
    cols = jnp.tile(jnp.arange(GRID_W), rows_n)
    pos = jnp.stack([rows, cols], axis=-1).astype(F32)
    half = A_HEAD_DIM // 2
    inv = ROPE_THETA ** (-jnp.arange(0, half, 2, dtype=F32) / half)
    ang = pos[:, :, None] * inv
    cos, sin = jnp.cos(ang), jnp.sin(ang)
    c_tab = jnp.concatenate([cos[:, 0], cos[:, 0], cos[:, 1], cos[:, 1]], axis=-1)
    s_tab = jnp.concatenate([-sin[:, 0], sin[:, 0], -sin[:, 1], sin[:, 1]], axis=-1)
    c_tab = jnp.concatenate([jnp.ones((n_ctx, A_HEAD_DIM), F32), c_tab], axis=0)
    s_tab = jnp.concatenate([jnp.zeros((n_ctx, A_HEAD_DIM), F32), s_tab], axis=0)
    return c_tab, s_tab


def _ssd_expand_matrix():
    lane = jnp.arange(128)[:, None]
    head = jnp.arange(C_INNER)[None, :] // C_HEAD_DIM
    return jnp.stack([(lane == SMALL_DT + 16 * d + head) for d in range(2)]).astype(BF16)


def _lane_row(vals):
    row = jnp.zeros((128,), F32)
    row = lax.dynamic_update_slice(row, vals.reshape(-1).astype(F32), (SMALL_DT,))
    return row.reshape(1, 128)


def kernel(x, c, ctx, c_ctx, w_ada, b_ada, norm_g, ffn_w_in, ffn_w_out, w_in, qk_g, gla_w_gate, gla_b_gate,
           gla_norm_g, ssd_conv_w, ssd_conv_b, ssd_a_log, ssd_dt_bias, ssd_d, ssd_norm_g, w_br, w_out):
    assert x.shape[0] == 1 and ctx.shape[0] == 1
    n_tok, d = x.shape[1], x.shape[2]
    n_ctx = ctx.shape[1]
    assert n_ctx == ROW_TILE and n_tok % ROW_TILE == 0 and d == D_MODEL
    depth = w_ada.shape[0]

    xs = jnp.concatenate([ctx[0], x[0]], axis=0)
    cc = jnp.zeros((16, d), F32).at[0].set(c_ctx).at[1].set(c[0])
    mods = _ada_call(cc, w_ada, b_ada).reshape(depth, 16, N_MOD, 1, d)

    w_in_p = _repack_w_in(w_in)
    rope_c, rope_s = _rope_tables(n_ctx, n_tok)
    e_mat = _ssd_expand_matrix()
    conv_w = jnp.pad(ssd_conv_w, ((0, 0), (0, 8 - C_CONV), (0, 0)))
    conv_b = ssd_conv_b.reshape(depth, 1, C_XBC)
    d_e = jnp.repeat(ssd_d, C_HEAD_DIM, axis=-1).reshape(depth, 1, C_INNER)
    wg_p = jnp.zeros((depth, 2, 128, B_QK_WIDTH), F32)
    wg_p = wg_p.at[:, 0, 0:B_RANK].set(gla_w_gate[:, 0]).at[:, 1, B_RANK:2 * B_RANK].set(gla_w_gate[:, 1])
    bg = gla_b_gate.reshape(depth, 2, 1, B_QK_WIDTH)

    for l in range(depth):
        h = _norm_mod_call(xs, norm_g, mods, l, 0, 0, 1)
        hf = _ffn_in_call(h, ffn_w_in, l, 0)
        xs = _mm_res_call(hf, (None, None), (l, 0), ffn_w_out, xs, mods, l, 2, 0.5, n_ctx, (384, 256), 512, "ffn_out")

        h = _norm_mod_call(xs, norm_g, mods, l, 1, 3, 4)
        u = _proj_call(h, w_in_p, l)
        q, k, v = _qk_prep_call(u, rope_c, rope_s, qk_g, l)
        o_a = _attn_call(q, k, v, n_ctx)
        of_b, ob_b = _gla_call(u, wg_p[l], bg[l])
        o_b = _gla_out_call(of_b, ob_b, u, gla_norm_g, l)
        xc = _ssd_conv_call(u, conv_w, conv_b, l)
        aneg_p = _lane_row(-jnp.exp(ssd_a_log[l].astype(F32)))
        dtb_p = _lane_row(ssd_dt_bias[l])
        yf_s, yb_s = _ssd_call(xc, u, dtb_p, aneg_p, e_mat)
        o_s = _ssd_out_call(yf_s, yb_s, xc, u, d_e, ssd_norm_g, l)
        m = _merge_call(o_a, o_b, o_s, u, w_br, l)
        xs = _mm_res_call(m, (None,), (l,), w_out, xs, mods, l, 5, 1.0, n_ctx, (768, 512, 256), 1024, "mix_out")

        h = _norm_mod_call(xs, norm_g, mods, l, 2, 6, 7)
        hf = _ffn_in_call(h, ffn_w_in, l, 1)
        xs = _mm_res_call(hf, (None, None), (l, 1), ffn_w_out, xs, mods, l, 8, 0.5, n_ctx, (384, 256), 512, "ffn_out")

    return xs[n_ctx:][None]
```

```python
import functools
import math

import jax
import jax.numpy as jnp
from jax import lax
from jax.experimental import pallas as pl
from jax.experimental.pallas import tpu as pltpu

F32 = jnp.float32
BF16 = jnp.bfloat16

D_MODEL = 2048
DEPTH = 4
GRID_W = 64
EPS = 1e-6
N_MOD = 9
D_FF = 5632

A_HEADS = 8
A_KV_HEADS = 2
A_HEAD_DIM = 128
ROPE_THETA = 10000.0
A_WIDTH = A_HEADS * A_HEAD_DIM
A_KV_WIDTH = A_KV_HEADS * A_HEAD_DIM

B_HEADS = 4
B_DK = 128
B_DV = 256
B_RANK = 16
B_TAU = 16.0
B_CHUNK = 64
B_QK_WIDTH = B_HEADS * B_DK
B_WIDTH = B_HEADS * B_DV

C_HEADS = 16
C_HEAD_DIM = 64
C_GROUPS = 2
C_STATE = 128
C_CONV = 5
C_CHUNK = 128
C_INNER = C_HEADS * C_HEAD_DIM
C_XBC = C_INNER + 2 * C_GROUPS * C_STATE
N_BRANCH = 3

W_AKV = 1024
W_BQ = 1536
W_BK = 2048
W_BV = 2560
W_BR = 3584
REG_A = (0, 4608, 0)
REG_B = (4608, 2560, 32)
REG_C = (7168, 6144, 64)
SMALL_DT = 32

ROW_TILE = 256
VMEM_LIMIT = 56 * 1024 * 1024


def _cparams(n_axes):
    return pltpu.CompilerParams(dimension_semantics=("arbitrary",) * n_axes, vmem_limit_bytes=VMEM_LIMIT)


def _pick_tile(total, candidates):
    for c in candidates:
        if total % c == 0:
            return c
    raise ValueError(f"no tile for {total}")


def _silu(x):
    return x * (1.0 / (1.0 + jnp.exp(-x)))


def _sigmoid(x):
    return 1.0 / (1.0 + jnp.exp(-x))


def _split3(x):
    hi = x.astype(BF16)
    r1 = x - hi.astype(F32)
    mid = r1.astype(BF16)
    lo = (r1 - mid.astype(F32)).astype(BF16)
    return hi, mid, lo


def _dot(a, b):
    return jnp.dot(a, b, preferred_element_type=F32)


def _dot_exact_lhs(sel, x):
    hi, mid, lo = _split3(x)
    return _dot(sel, hi) + _dot(sel, mid) + _dot(sel, lo)


def _dot_exact_rhs(x, sel):
    hi, mid, lo = _split3(x)
    return _dot(hi, sel) + _dot(mid, sel) + _dot(lo, sel)


def _ada_kernel(c_ref, w_ref, b_ref, o_ref):
    a = _silu(c_ref[...]).astype(BF16)
    o_ref[...] = _dot(a, w_ref[...].astype(BF16)) + b_ref[...]


def _ada_call(cc, w_ada, b_ada):
    depth, d, n = w_ada.shape
    tn = 1024
    return pl.pallas_call(
        _ada_kernel,
        grid=(depth, n // tn),
        in_specs=[
            pl.BlockSpec((16, d), lambda l, j: (0, 0)),
            pl.BlockSpec((None, d, tn), lambda l, j: (l, 0, j)),
            pl.BlockSpec((None, 1, tn), lambda l, j: (l, 0, j)),
        ],
        out_specs=pl.BlockSpec((None, 16, tn), lambda l, j: (l, 0, j)),
        out_shape=jax.ShapeDtypeStruct((depth, 16, n), F32),
        compiler_params=_cparams(2),
        name="ada_mod",
    )(cc, w_ada, b_ada.reshape(depth, 1, n))


def _norm_mod_kernel(x_ref, g_ref, sh_ref, sc_ref, o_ref):
    x = x_ref[...]
    y = x * lax.rsqrt(jnp.mean(x * x, axis=-1, keepdims=True) + EPS) * g_ref[...]
    o_ref[...] = (y * (1.0 + sc_ref[...]) + sh_ref[...]).astype(o_ref.dtype)


def _norm_mod_call(xs, norm_g, mods, l, gi, shift_i, scale_i):
    t, d = xs.shape
    tm = ROW_TILE
    row = lambda i: jnp.minimum(i, 1)
    return pl.pallas_call(
        _norm_mod_kernel,
        grid=(t // tm,),
        in_specs=[
            pl.BlockSpec((tm, d), lambda i: (i, 0)),
            pl.BlockSpec((None, None, 1, d), lambda i: (l, gi, 0, 0)),
            pl.BlockSpec((None, None, None, 1, d), lambda i: (l, row(i), shift_i, 0, 0)),
            pl.BlockSpec((None, None, None, 1, d), lambda i: (l, row(i), scale_i, 0, 0)),
        ],
        out_specs=pl.BlockSpec((tm, d), lambda i: (i, 0)),
        out_shape=jax.ShapeDtypeStruct((t, d), BF16),
        compiler_params=_cparams(1),
        name="norm_mod",
    )(xs, norm_g.reshape(norm_g.shape[0], 3, 1, d), mods, mods)


def _ffn_in_kernel(a_ref, wa_ref, wu_ref, o_ref, wa_bf, wu_bf):
    @pl.when(pl.program_id(1) == 0)
    def _():
        wa_bf[...] = wa_ref[...].astype(BF16)
        wu_bf[...] = wu_ref[...].astype(BF16)

    a = a_ref[...]
    ga = _dot(a, wa_bf[...])
    gu = _dot(a, wu_bf[...])
    o_ref[...] = (_silu(ga) * gu).astype(o_ref.dtype)


def _ffn_in_call(h, ffn_w_in, l, f):
    t, d = h.shape
    tn = 512
    tm = _pick_tile(t, (768, 512, 256))
    nj = D_FF // tn
    return pl.pallas_call(
        _ffn_in_kernel,
        grid=(nj, t // tm),
        in_specs=[
            pl.BlockSpec((tm, d), lambda j, i: (i, 0)),
            pl.BlockSpec((None, None, d, tn), lambda j, i: (l, f, 0, j)),
            pl.BlockSpec((None, None, d, tn), lambda j, i: (l, f, 0, j + nj)),
        ],
        out_specs=pl.BlockSpec((tm, tn), lambda j, i: (i, j)),
        out_shape=jax.ShapeDtypeStruct((t, D_FF), BF16),
        scratch_shapes=[pltpu.VMEM((d, tn), BF16), pltpu.VMEM((d, tn), BF16)],
        compiler_params=_cparams(2),
        name="ffn_in",
    )(h, ffn_w_in, ffn_w_in)


def _mm_res_kernel(a_ref, w_ref, res_ref, gate_ref, o_ref, w_bf, *, factor, tm, n_ctx):
    @pl.when(pl.program_id(1) == 0)
    def _():
        w_bf[...] = w_ref[...].astype(BF16)

    acc = _dot(a_ref[...], w_bf[...])
    rows = pl.program_id(1) * tm + lax.broadcasted_iota(jnp.int32, (tm, 1), 0)
    gate = jnp.where(rows < n_ctx, gate_ref[0], gate_ref[1])
    if factor != 1.0:
        gate = factor * gate
    o_ref[...] = res_ref[...] + gate * acc


def _mm_res_call(a, w_spec_shape, w_index, w, xs, mods, l, gate_i, factor, n_ctx, tm_cands, tn, name):
    t, k = a.shape
    d = xs.shape[1]
    tm = _pick_tile(t, tm_cands)
    return pl.pallas_call(
        functools.partial(_mm_res_kernel, factor=factor, tm=tm, n_ctx=n_ctx),
        grid=(d // tn, t // tm),
        in_specs=[
            pl.BlockSpec((tm, k), lambda j, i: (i, 0)),
            pl.BlockSpec(w_spec_shape + (k, tn), lambda j, i: w_index + (0, j)),
            pl.BlockSpec((tm, tn), lambda j, i: (i, j)),
            pl.BlockSpec((None, 2, None, 1, tn), lambda j, i: (l, 0, gate_i, 0, j)),
        ],
        out_specs=pl.BlockSpec((tm, tn), lambda j, i: (i, j)),
        out_shape=jax.ShapeDtypeStruct((t, d), F32),
        scratch_shapes=[pltpu.VMEM((k, tn), BF16)],
        compiler_params=_cparams(2),
        name=name,
    )(a, w, xs, mods)


def _proj_kernel(a_ref, w_ref, o_ref, w_bf):
    @pl.when(pl.program_id(1) == 0)
    def _():
        w_bf[...] = w_ref[...].astype(BF16)

    o_ref[...] = _dot(a_ref[...], w_bf[...])


def _proj_shift_kernel(a_ref, w_ref, wt_ref, *rest, shift, small_steps):
    if small_steps:
        o_ref, sm_ref, w_bf, wsm_bf = rest
    else:
        o_ref, w_bf = rest
    d, tn = w_ref.shape
    j = pl.program_id(0)

    @pl.when(pl.program_id(1) == 0)
    def _():
        for r in range(0, d, 256):
            wcat = jnp.concatenate([w_ref[r:r + 256, :], wt_ref[r:r + 256, :]], axis=1)
            w_bf[r:r + 256, :] = pltpu.roll(wcat, tn + 128 - shift, 1)[:, :tn].astype(BF16)
        if small_steps:
            wsm_bf[...] = jnp.where(j == small_steps[0], w_ref[:, 0:128], wt_ref[...]).astype(BF16)

    a = a_ref[...]
    o_ref[...] = _dot(a, w_bf[...])
    if small_steps:
        is_small = (j == small_steps[0]) | (j == small_steps[1])

        @pl.when(is_small)
        def _():
            sm_ref[...] = _dot(a, wsm_bf[...])

        @pl.when(jnp.logical_not(is_small))
        def _():
            sm_ref[...] = jnp.zeros_like(sm_ref)


def _proj_call(h, w_in, l, region, tn, with_small=False):
    t, d = h.shape
    start, width, shift = region
    tm = _pick_tile(t, (768, 512, 256))
    nj = width // tn
    assert start % tn == 0 and width % tn == 0
    in_specs = [
        pl.BlockSpec((tm, d), lambda j, i: (i, 0)),
        pl.BlockSpec((None, d, tn), lambda j, i: (l, 0, start // tn + j)),
    ]
    out_specs = [pl.BlockSpec((tm, tn), lambda j, i: (i, j))]
    out_shape = [jax.ShapeDtypeStruct((t, width), F32)]
    scratch = [pltpu.VMEM((d, tn), BF16)]
    args = [h, w_in]
    if shift == 0:
        body = _proj_kernel
    else:
        small_steps = (0, nj - 1) if with_small else ()
        body = functools.partial(_proj_shift_kernel, shift=shift, small_steps=small_steps)
        in_specs.append(pl.BlockSpec((None, d, 128), lambda j, i: (l, 0, start // 128 + (j + 1) * (tn // 128))))
        args.append(w_in)
        if with_small:
            out_specs.append(pl.BlockSpec((tm, 128), lambda j, i: (i, j)))
            out_shape.append(jax.ShapeDtypeStruct((t, nj * 128), F32))
            scratch.append(pltpu.VMEM((d, 128), BF16))
    return pl.pallas_call(
        body,
        grid=(nj, t // tm),
        in_specs=in_specs,
        out_specs=out_specs,
        out_shape=out_shape,
        scratch_shapes=scratch,
        compiler_params=_cparams(2),
        name="in_proj",
    )(*args)


def _qk_prep_kernel(q_in, kv_in, cos_ref, sin_ref, g_ref, q_ref, k_ref, v_ref):
    tm = q_in.shape[0]
    cs = cos_ref[...]
    sn = sin_ref[...]
    lane = lax.broadcasted_iota(jnp.int32, (tm, A_HEAD_DIM), 1)
    first = (lane % 64) < 32
    c2 = (A_HEAD_DIM ** -0.5) * math.log2(math.e)

    def norm_rope(x, g):
        y = x * lax.rsqrt(jnp.mean(x * x, axis=-1, keepdims=True) + EPS) * g
        partner = jnp.where(first, pltpu.roll(y, 96, 1), pltpu.roll(y, 32, 1))
        return y * cs + partner * sn

    for h in range(A_HEADS):
        q_ref[h] = (norm_rope(q_in[:, h * 128:(h + 1) * 128], g_ref[0:1, :]) * c2).astype(BF16)
    for h in range(A_KV_HEADS):
        k_ref[h] = norm_rope(kv_in[:, h * 128:(h + 1) * 128], g_ref[1:2, :]).astype(BF16)
        v_ref[h, :, 0:A_HEAD_DIM] = kv_in[:, A_KV_WIDTH + h * 128:A_KV_WIDTH + (h + 1) * 128].astype(BF16)
        v_ref[h, :, A_HEAD_DIM:2 * A_HEAD_DIM] = jnp.ones((tm, A_HEAD_DIM), BF16)


def _qk_prep_call(u_a, rope_c, rope_s, qk_g, l):
    t = u_a.shape[0]
    tm = ROW_TILE
    return pl.pallas_call(
        _qk_prep_kernel,
        grid=(t // tm,),
        in_specs=[
            pl.BlockSpec((tm, A_WIDTH), lambda i: (i, 0)),
            pl.BlockSpec((tm, 2 * A_KV_WIDTH), lambda i: (i, W_AKV // (2 * A_KV_WIDTH))),
            pl.BlockSpec((tm, A_HEAD_DIM), lambda i: (i, 0)),
            pl.BlockSpec((tm, A_HEAD_DIM), lambda i: (i, 0)),
            pl.BlockSpec((None, 2, A_HEAD_DIM), lambda i: (l, 0, 0)),
        ],
        out_specs=[
            pl.BlockSpec((A_HEADS, tm, A_HEAD_DIM), lambda i: (0, i, 0)),
            pl.BlockSpec((A_KV_HEADS, tm, A_HEAD_DIM), lambda i: (0, i, 0)),
            pl.BlockSpec((A_KV_HEADS, tm, 2 * A_HEAD_DIM), lambda i: (0, i, 0)),
        ],
        out_shape=[
            jax.ShapeDtypeStruct((A_HEADS, t, A_HEAD_DIM), BF16),
            jax.ShapeDtypeStruct((A_KV_HEADS, t, A_HEAD_DIM), BF16),
            jax.ShapeDtypeStruct((A_KV_HEADS, t, 2 * A_HEAD_DIM), BF16),
        ],
        compiler_params=_cparams(1),
        name="qk_prep",
    )(u_a, u_a, rope_c, rope_s, qk_g)


def _attn_kernel(q_ref, k_ref, v_ref, o_ref, m_sc, acc_sc, sa_sc, sb_sc, pa_sc, pb_sc, *, tq, n_ctx, t_all, ck):
    grp = A_HEADS // A_KV_HEADS
    rows = grp * tq
    dh = A_HEAD_DIM
    rb = 64
    n_chunks = t_all // ck
    q = q_ref[...].reshape(rows, dh)
    contract_last = (((1,), (1,)), ((), ()))

    def write_out(out):
        for h in range(grp):
            o_ref[:, h * dh:(h + 1) * dh] = out[h * tq:(h + 1) * tq].astype(o_ref.dtype)

    def chunk_rows(c):
        return pl.ds(c * ck if isinstance(c, int) else pl.multiple_of(c * ck, ck), ck)

    def qk(s_sc, c):
        s_sc[...] = lax.dot_general(q, k_ref[chunk_rows(c), :], contract_last, preferred_element_type=F32)

    def softmax_pv(s_sc, p_sc, c):
        for r in range(rows // rb):
            rs = slice(r * rb, (r + 1) * rb)
            s = s_sc[rs, :]
            m_prev = m_sc[rs, :]
            m_next = jnp.maximum(m_prev, jnp.max(s, axis=-1, keepdims=True))
            p_sc[rs, :] = jnp.exp2(s - pltpu.repeat(m_next, ck // dh, 1)).astype(BF16)
            acc_sc[rs, :] = pltpu.repeat(jnp.exp2(m_prev - m_next), 2, 1) * acc_sc[rs, :]
            m_sc[rs, :] = m_next
        acc_sc[...] += _dot(p_sc[...], v_ref[chunk_rows(c), :])

    i = pl.program_id(1)

    @pl.when(i == 0)
    def _():
        s = lax.dot_general(q, k_ref[0:n_ctx, :], contract_last, preferred_element_type=F32)
        p = jnp.exp2(s - jnp.max(s, axis=-1, keepdims=True)).astype(BF16)
        acc = _dot(p, v_ref[0:n_ctx, :])
        write_out(acc[:, 0:dh] / acc[:, dh:2 * dh])

    @pl.when(i > 0)
    def _():
        m_sc[...] = jnp.full((rows, dh), -jnp.inf, F32)
        acc_sc[...] = jnp.zeros((rows, 2 * dh), F32)
        qk(sa_sc, 0)

        def body(it, carry):
            c = 2 * it
            qk(sb_sc, c + 1)
            softmax_pv(sa_sc, pa_sc, c)
            qk(sa_sc, c + 2)
            softmax_pv(sb_sc, pb_sc, c + 1)
            return carry

        lax.fori_loop(0, (n_chunks - 1) // 2, body, 0)
        if n_chunks % 2 == 0:
            qk(sb_sc, n_chunks - 1)
            softmax_pv(sa_sc, pa_sc, n_chunks - 2)
            softmax_pv(sb_sc, pb_sc, n_chunks - 1)
        else:
            softmax_pv(sa_sc, pa_sc, n_chunks - 1)
        write_out(acc_sc[:, 0:dh] / acc_sc[:, dh:2 * dh])


def _attn_call(q, k, v, n_ctx):
    _, t, dh = q.shape
    tq = ROW_TILE
    grp = A_HEADS // A_KV_HEADS
    ck = _pick_tile(t, (768, 512, 256))
    return pl.pallas_call(
        functools.partial(_attn_kernel, tq=tq, n_ctx=n_ctx, t_all=t, ck=ck),
        grid=(A_KV_HEADS, t // tq),
        in_specs=[
            pl.BlockSpec((grp, tq, dh), lambda g, i: (g, i, 0)),
            pl.BlockSpec((None, t, dh), lambda g, i: (g, 0, 0)),
            pl.BlockSpec((None, t, 2 * dh), lambda g, i: (g, 0, 0)),
        ],
        out_specs=pl.BlockSpec((tq, grp * dh), lambda g, i: (i, g)),
        out_shape=jax.ShapeDtypeStruct((t, A_WIDTH), BF16),
        scratch_shapes=[
            pltpu.VMEM((grp * tq, dh), F32),
            pltpu.VMEM((grp * tq, 2 * dh), F32),
            pltpu.VMEM((grp * tq, ck), F32),
            pltpu.VMEM((grp * tq, ck), F32),
            pltpu.VMEM((grp * tq, ck), BF16),
            pltpu.VMEM((grp * tq, ck), BF16),
        ],
        compiler_params=_cparams(2),
        name="attn",
    )(q, k, v)


def _bwd_block(i, nb):
    return jnp.where(i == 0, 0, nb - i)


def _gla_dir(q_ref, k_ref, v01_ref, v23_ref, sm_ref, wg_ref, bg_ref, o_ref, st_ref, reverse):
    tm = q_ref.shape[0]
    nch = tm // B_CHUNK
    r = lax.broadcasted_iota(jnp.int32, (tm, tm), 0)
    c = lax.broadcasted_iota(jnp.int32, (tm, tm), 1)
    same = (r // B_CHUNK) == (c // B_CHUNK)
    tri = same & ((c >= r) if reverse else (c <= r))
    tri_bf = tri.astype(BF16)
    same_bf = same.astype(BF16)

    pre = jnp.dot(sm_ref[...], wg_ref[...], precision=lax.Precision.HIGHEST, preferred_element_type=F32) + bg_ref[...]
    la = (jnp.minimum(pre, 0.0) - jnp.log1p(jnp.exp(-jnp.abs(pre)))) / B_TAU
    hi, mid, lo = _split3(la)
    b = _dot(tri_bf, hi) + _dot(tri_bf, mid) + _dot(tri_bf, lo)
    tot = _dot(same_bf, hi) + _dot(same_bf, mid) + _dot(same_bf, lo)
    eb = jnp.exp(b)
    qf = (q_ref[...] * (B_DK ** -0.5)) * eb
    kf = k_ref[...] * jnp.exp(-b)
    k_end = k_ref[...] * jnp.exp(tot - b)
    dec = jnp.exp(tot)
    v_pairs = (v01_ref[...].astype(BF16), v23_ref[...].astype(BF16))
    qf_bf = qf.astype(BF16)
    kf_bf = kf.astype(BF16)
    ke_bf = k_end.astype(BF16)

    order = range(nch - 1, -1, -1) if reverse else range(nch)
    for h in range(B_HEADS):
        ks = slice(h * B_DK, (h + 1) * B_DK)
        vs = slice(h * B_DV, (h + 1) * B_DV)
        att = lax.dot_general(qf_bf[:, ks], kf_bf[:, ks], (((1,), (1,)), ((), ())), preferred_element_type=F32)
        att = jnp.where(tri, att, 0.0).astype(BF16)
        v = v_pairs[h // 2][:, (h % 2) * B_DV:(h % 2 + 1) * B_DV]
        o_intra = _dot(att, v)
        for j in order:
            rs = slice(j * B_CHUNK, (j + 1) * B_CHUNK)
            st = st_ref[h]
            o_inter = lax.dot_general(qf_bf[rs, ks], st.astype(BF16), (((1,), (1,)), ((), ())),
                                      preferred_element_type=F32)
            o_ref[rs, vs] = o_intra[rs] + o_inter
            ut = lax.dot_general(v[rs], ke_bf[rs, ks], (((0,), (0,)), ((), ())), preferred_element_type=F32)
            st_ref[h] = dec[j * B_CHUNK:j * B_CHUNK + 1, ks] * st + ut


def _gla_kernel(qf_ref, kf_ref, vf01_ref, vf23_ref, sf_ref, qb_ref, kb_ref, vb01_ref, vb23_ref, sb_ref, wg_ref, bg_ref,
                of_ref, ob_ref, stf_ref, stb_ref):
    @pl.when(pl.program_id(0) == 0)
    def _():
        stf_ref[...] = jnp.zeros_like(stf_ref)
        stb_ref[...] = jnp.zeros_like(stb_ref)

    _gla_dir(qf_ref, kf_ref, vf01_ref, vf23_ref, sf_ref, wg_ref.at[0], bg_ref.at[0], of_ref, stf_ref, False)
    _gla_dir(qb_ref, kb_ref, vb01_ref, vb23_ref, sb_ref, wg_ref.at[1], bg_ref.at[1], ob_ref, stb_ref, True)


def _gla_call(u_a, sm, wg_p, bg):
    t = u_a.shape[0]
    tm = ROW_TILE
    nb = t // tm
    fwd = lambda i: i
    bwd = lambda i: _bwd_block(i, nb)

    def views(blk):
        return [
            pl.BlockSpec((tm, B_QK_WIDTH), lambda i: (blk(i), W_BQ // B_QK_WIDTH)),
            pl.BlockSpec((tm, B_QK_WIDTH), lambda i: (blk(i), W_BK // B_QK_WIDTH)),
            pl.BlockSpec((tm, 2 * B_DV), lambda i: (blk(i), W_BV // (2 * B_DV))),
            pl.BlockSpec((tm, 2 * B_DV), lambda i: (blk(i), W_BV // (2 * B_DV) + 1)),
            pl.BlockSpec((tm, 128), lambda i: (blk(i), 0)),
        ]

    return pl.pallas_call(
        _gla_kernel,
        grid=(nb,),
        in_specs=views(fwd) + views(bwd) + [
            pl.BlockSpec((2, 128, B_QK_WIDTH), lambda i: (0, 0, 0)),
            pl.BlockSpec((2, 1, B_QK_WIDTH), lambda i: (0, 0, 0)),
        ],
        out_specs=[
            pl.BlockSpec((tm, B_WIDTH), lambda i: (fwd(i), 0)),
            pl.BlockSpec((tm, B_WIDTH), lambda i: (bwd(i), 0)),
        ],
        out_shape=[jax.ShapeDtypeStruct((t, B_WIDTH), F32), jax.ShapeDtypeStruct((t, B_WIDTH), F32)],
        scratch_shapes=[pltpu.VMEM((B_HEADS, B_DV, B_DK), F32), pltpu.VMEM((B_HEADS, B_DV, B_DK), F32)],
        compiler_params=_cparams(1),
        name="gla_scan",
    )(u_a, u_a, u_a, u_a, sm, u_a, u_a, u_a, u_a, sm, wg_p, bg)


def _gla_out_kernel(of_ref, ob_ref, r01_ref, r23_ref, g_ref, y_ref):
    for h in range(B_HEADS):
        vs = slice(h * B_DV, (h + 1) * B_DV)
        r_ref = r01_ref if h < 2 else r23_ref
        o = of_ref[:, vs] + ob_ref[:, vs]
        y = o * lax.rsqrt(jnp.mean(o * o, axis=-1, keepdims=True) + EPS) * g_ref[...]
        y_ref[:, vs] = (y * _silu(r_ref[:, (h % 2) * B_DV:(h % 2 + 1) * B_DV])).astype(y_ref.dtype)


def _gla_out_call(o_f, o_b, u_a, gla_norm_g, l):
    t = u_a.shape[0]
    tm = ROW_TILE
    return pl.pallas_call(
        _gla_out_kernel,
        grid=(t // tm,),
        in_specs=[
            pl.BlockSpec((tm, B_WIDTH), lambda i: (i, 0)),
            pl.BlockSpec((tm, B_WIDTH), lambda i: (i, 0)),
            pl.BlockSpec((tm, 2 * B_DV), lambda i: (i, W_BR // (2 * B_DV))),
            pl.BlockSpec((tm, 2 * B_DV), lambda i: (i, W_BR // (2 * B_DV) + 1)),
            pl.BlockSpec((None, 1, B_DV), lambda i: (l, 0, 0)),
        ],
        out_specs=pl.BlockSpec((tm, B_WIDTH), lambda i: (i, 0)),
        out_shape=jax.ShapeDtypeStruct((t, B_WIDTH), BF16),
        compiler_params=_cparams(1),
        name="gla_out",
    )(o_f, o_b, u_a, u_a, gla_norm_g.reshape(gla_norm_g.shape[0], 1, B_DV))


def _ssd_conv_kernel(prev_ref, cur_ref, next_ref, w_ref, b_ref, o_ref, *, nb):
    i = pl.program_id(0)
    tm = cur_ref.shape[0]
    prev_ok = (i >= 2).astype(F32)
    next_ok = ((i >= 1) & (i < nb - 1)).astype(F32)
    ext = jnp.concatenate([prev_ref[...] * prev_ok, cur_ref[...], next_ref[...] * next_ok], axis=0)
    n_ext = tm + 16
    acc = jnp.zeros((tm, C_XBC), F32) + b_ref[...]
    for j in range(C_CONV):
        shift = (C_CONV // 2 - j) % n_ext
        sh = ext if shift == 0 else pltpu.roll(ext, shift, 0)
        acc = acc + sh[8:8 + tm, :] * w_ref[j:j + 1, :]
    o_ref[...] = _silu(acc)


def _ssd_conv_call(u, conv_w, conv_b, l):
    t = u.shape[0]
    tm = ROW_TILE
    nb = t // tm
    r8 = tm // 8
    cb = 0
    return pl.pallas_call(
        functools.partial(_ssd_conv_kernel, nb=nb),
        grid=(nb,),
        in_specs=[
            pl.BlockSpec((8, C_XBC), lambda i: (jnp.maximum(i * r8 - 1, 0), cb)),
            pl.BlockSpec((tm, C_XBC), lambda i: (i, cb)),
            pl.BlockSpec((8, C_XBC), lambda i: (jnp.minimum((i + 1) * r8, nb * r8 - 1), cb)),
            pl.BlockSpec((None, 8, C_XBC), lambda i: (l, 0, 0)),
            pl.BlockSpec((None, 1, C_XBC), lambda i: (l, 0, 0)),
        ],
        out_specs=pl.BlockSpec((tm, C_XBC), lambda i: (i, 0)),
        out_shape=jax.ShapeDtypeStruct((t, C_XBC), F32),
        compiler_params=_cparams(1),
        name="ssd_conv",
    )(u, u, u, conv_w, conv_b)


def _ssd_dir(xc_ref, sm_ref, dtb_ref, aneg_ref, e_ref, y_ref, st_ref, d, reverse):
    tm = xc_ref.shape[0]
    nch = tm // C_CHUNK
    cc = C_CHUNK
    r = lax.broadcasted_iota(jnp.int32, (cc, cc), 0)
    c = lax.broadcasted_iota(jnp.int32, (cc, cc), 1)
    tri = (c >= r) if reverse else (c <= r)
    tri_bf = tri.astype(BF16)
    ones_bf = jnp.ones((cc, cc), BF16)
    lane = lax.broadcasted_iota(jnp.int32, (cc, 128), 1)
    left = lane < C_HEAD_DIM
    e_bf = e_ref[...]

    order = range(nch - 1, -1, -1) if reverse else range(nch)
    for j in order:
        rs = slice(j * cc, (j + 1) * cc)
        x = xc_ref[rs, 0:C_INNER]
        dt_raw = sm_ref[rs, :] + dtb_ref[...]
        dt = jnp.maximum(dt_raw, 0.0) + jnp.log1p(jnp.exp(-jnp.abs(dt_raw)))
        a = dt * aneg_ref[...]
        hi, mid, lo = _split3(a)
        cum = _dot(tri_bf, hi) + _dot(tri_bf, mid) + _dot(tri_bf, lo)
        tot = _dot(ones_bf, hi) + _dot(ones_bf, mid) + _dot(ones_bf, lo)
        cum_t = cum.T
        dt_e = _dot_exact_rhs(dt, e_bf)
        cum_e = _dot_exact_rhs(cum, e_bf)
        tot_e = _dot_exact_rhs(tot, e_bf)
        xd = x * dt_e
        xw_bf = (xd * jnp.exp(tot_e - cum_e)).astype(BF16)
        xd_bf = xd.astype(BF16)
        ec_e = jnp.exp(cum_e)
        dec_e = jnp.exp(tot_e[0:1, :])
        for g in range(C_GROUPS):
            bm = xc_ref[rs, C_INNER + g * C_STATE:C_INNER + (g + 1) * C_STATE]
            cm = xc_ref[rs, C_INNER + (C_GROUPS + g) * C_STATE:C_INNER + (C_GROUPS + g + 1) * C_STATE]
            bm_bf = bm.astype(BF16)
            cm_bf = cm.astype(BF16)
            cb = lax.dot_general(cm_bf, bm_bf, (((1,), (1,)), ((), ())), preferred_element_type=F32)
            bt_bf = bm.T.astype(BF16)
            for pr in range(C_HEADS // C_GROUPS // 2):
                pair = g * (C_HEADS // C_GROUPS // 2) + pr
                ls = slice(pair * 128, (pair + 1) * 128)
                y_pair = None
                for half in range(2):
                    hl = SMALL_DT + 16 * d + 2 * pair + half
                    col = jnp.broadcast_to(cum[:, hl:hl + 1], (cc, cc))
                    row = jnp.broadcast_to(cum_t[hl:hl + 1, :], (cc, cc))
                    w = (cb * jnp.where(tri, jnp.exp(col - row), 0.0)).astype(BF16)
                    xm = jnp.where(left if half == 0 else ~left, xd_bf[:, ls], jnp.zeros((), BF16))
                    part = _dot(w, xm)
                    y_pair = part if y_pair is None else y_pair + part
                st = st_ref[pair]
                y_ref[rs, ls] = y_pair + _dot(cm_bf, st.astype(BF16)) * ec_e[:, ls]
                st_ref[pair] = dec_e[:, ls] * st + _dot(bt_bf, xw_bf[:, ls])


def _ssd_kernel(xf_ref, sf_ref, xb_ref, sb_ref, dtb_ref, aneg_ref, e_ref, yf_ref, yb_ref, stf_ref, stb_ref):
    @pl.when(pl.program_id(0) == 0)
    def _():
        stf_ref[...] = jnp.zeros_like(stf_ref)
        stb_ref[...] = jnp.zeros_like(stb_ref)

    _ssd_dir(xf_ref, sf_ref, dtb_ref, aneg_ref, e_ref.at[0], yf_ref, stf_ref, 0, False)
    _ssd_dir(xb_ref, sb_ref, dtb_ref, aneg_ref, e_ref.at[1], yb_ref, stb_ref, 1, True)


def _ssd_call(xc, sm, dtb_p, aneg_p, e_mat):
    t = xc.shape[0]
    tm = ROW_TILE
    nb = t // tm
    dt_blk = sm.shape[1] // 128 - 1
    fwd = lambda i: i
    bwd = lambda i: _bwd_block(i, nb)

    def views(blk):
        return [
            pl.BlockSpec((tm, C_XBC), lambda i: (blk(i), 0)),
            pl.BlockSpec((tm, 128), lambda i: (blk(i), dt_blk)),
        ]

    return pl.pallas_call(
        _ssd_kernel,
        grid=(nb,),
        in_specs=views(fwd) + views(bwd) + [
            pl.BlockSpec((1, 128), lambda i: (0, 0)),
            pl.BlockSpec((1, 128), lambda i: (0, 0)),
            pl.BlockSpec((2, 128, C_INNER), lambda i: (0, 0, 0)),
        ],
        out_specs=[
            pl.BlockSpec((tm, C_INNER), lambda i: (fwd(i), 0)),
            pl.BlockSpec((tm, C_INNER), lambda i: (bwd(i), 0)),
        ],
        out_shape=[jax.ShapeDtypeStruct((t, C_INNER), F32), jax.ShapeDtypeStruct((t, C_INNER), F32)],
        scratch_shapes=[pltpu.VMEM((C_HEADS // 2, C_STATE, 128), F32), pltpu.VMEM((C_HEADS // 2, C_STATE, 128), F32)],
        compiler_params=_cparams(1),
        name="ssd_scan",
    )(xc, sm, xc, sm, dtb_p, aneg_p, e_mat)


def _ssd_out_kernel(yf_ref, yb_ref, xc_ref, z0_ref, z1_ref, d_ref, g_ref, y_ref):
    gw = C_INNER // C_GROUPS
    for g, z_ref in enumerate((z0_ref, z1_ref)):
        ls = slice(g * gw, (g + 1) * gw)
        y = yf_ref[:, ls] + yb_ref[:, ls] + d_ref[:, ls] * xc_ref[:, ls]
        y = y * _silu(z_ref[...])
        y = y * lax.rsqrt(jnp.mean(y * y, axis=-1, keepdims=True) + EPS) * g_ref[:, ls]
        y_ref[:, ls] = y.astype(y_ref.dtype)


def _ssd_out_call(y_f, y_b, xc, u_b, d_e, ssd_norm_g, l):
    t = xc.shape[0]
    tm = ROW_TILE
    gw = C_INNER // C_GROUPS
    return pl.pallas_call(
        _ssd_out_kernel,
        grid=(t // tm,),
        in_specs=[
            pl.BlockSpec((tm, C_INNER), lambda i: (i, 0)),
            pl.BlockSpec((tm, C_INNER), lambda i: (i, 0)),
            pl.BlockSpec((tm, C_INNER), lambda i: (i, 0)),
            pl.BlockSpec((tm, gw), lambda i: (i, C_XBC // gw)),
            pl.BlockSpec((tm, gw), lambda i: (i, C_XBC // gw + 1)),
            pl.BlockSpec((None, 1, C_INNER), lambda i: (l, 0, 0)),
            pl.BlockSpec((None, 1, C_INNER), lambda i: (l, 0, 0)),
        ],
        out_specs=pl.BlockSpec((tm, C_INNER), lambda i: (i, 0)),
        out_shape=jax.ShapeDtypeStruct((t, C_INNER), BF16),
        compiler_params=_cparams(1),
        name="ssd_out",
    )(y_f, y_b, xc, u_b, u_b, d_e, ssd_norm_g.reshape(ssd_norm_g.shape[0], 1, C_INNER))


def _merge_kernel(oa_ref, ob_ref, os_ref, wa_ref, wb_ref, ws_ref, ga_ref, gb_ref, gs_ref, o_ref, w_bf):
    @pl.when(pl.program_id(1) == 0)
    def _():
        w_bf[0] = wa_ref[...].astype(BF16)
        w_bf[1] = wb_ref[...].astype(BF16)
        w_bf[2] = ws_ref[...].astype(BF16)

    m = _sigmoid(ga_ref[...]) * _dot(oa_ref[...], w_bf[0])
    m = m + _sigmoid(gb_ref[...]) * _dot(ob_ref[...], w_bf[1])
    m = m + _sigmoid(gs_ref[...]) * _dot(os_ref[...], w_bf[2])
    o_ref[...] = m.astype(o_ref.dtype)


def _merge_call(o_a, o_b, o_s, u, w_br, l):
    t = u.shape[0]
    tn = 512
    tm = _pick_tile(t, (768, 512, 256))
    k = A_WIDTH
    gb = 0
    nd = D_MODEL // tn
    a_spec = pl.BlockSpec((tm, k), lambda j, i: (i, 0))
    return pl.pallas_call(
        _merge_kernel,
        grid=(nd, t // tm),
        in_specs=[
            a_spec, a_spec, a_spec,
            pl.BlockSpec((None, k, tn), lambda j, i: (l, 0, j)),
            pl.BlockSpec((None, k, tn), lambda j, i: (l, 1, j)),
            pl.BlockSpec((None, k, tn), lambda j, i: (l, 2, j)),
            pl.BlockSpec((tm, tn), lambda j, i: (i, gb + j)),
            pl.BlockSpec((tm, tn), lambda j, i: (i, gb + nd + j)),
            pl.BlockSpec((tm, tn), lambda j, i: (i, gb + 2 * nd + j)),
        ],
        out_specs=pl.BlockSpec((tm, tn), lambda j, i: (i, j)),
        out_shape=jax.ShapeDtypeStruct((t, D_MODEL), BF16),
        scratch_shapes=[pltpu.VMEM((N_BRANCH, k, tn), BF16)],
        compiler_params=_cparams(2),
        name="branch_merge",
    )(o_a, o_b, o_s, w_br, w_br, w_br, u, u, u)


def _rope_tables(n_ctx, n_tok):
    rows_n = n_tok // GRID_W
    rows = jnp.repeat(jnp.arange(rows_n), GRID_W)
    cols = jnp.tile(jnp.arange(GRID_W), rows_n)
    pos = jnp.stack([rows, cols], axis=-1).astype(F32)
    half = A_HEAD_DIM // 2
    inv = ROPE_THETA ** (-jnp.arange(0, half, 2, dtype=F32) / half)
    ang = pos[:, :, None] * inv
    cos, sin = jnp.cos(ang), jnp.sin(ang)
    c_tab = jnp.concatenate([cos[:, 0], cos[:, 0], cos[:, 1], cos[:, 1]], axis=-1)
    s_tab = jnp.concatenate([-sin[:, 0], sin[:, 0], -sin[:, 1], sin[:, 1]], axis=-1)
    c_tab = jnp.concatenate([jnp.ones((n_ctx, A_HEAD_DIM), F32), c_tab], axis=0)
    s_tab = jnp.concatenate([jnp.zeros((n_ctx, A_HEAD_DIM), F32), s_tab], axis=0)
    return c_tab, s_tab


def _ssd_expand_matrix():
    lane = jnp.arange(128)[:, None]
    head = jnp.arange(C_INNER)[None, :] // C_HEAD_DIM
    return jnp.stack([(lane == SMALL_DT + 16 * d + head) for d in range(2)]).astype(BF16)


def _lane_row(vals):
    row = jnp.zeros((128,), F32)
    row = lax.dynamic_update_slice(row, vals.reshape(-1).astype(F32), (SMALL_DT,))
    return row.reshape(1, 128)


def kernel(x, c, ctx, c_ctx, w_ada, b_ada, norm_g, ffn_w_in, ffn_w_out, w_in, qk_g, gla_w_gate, gla_b_gate,
           gla_norm_g, ssd_conv_w, ssd_conv_b, ssd_a_log, ssd_dt_bias, ssd_d, ssd_norm_g, w_br, w_out):
    assert x.shape[0] == 1 and ctx.shape[0] == 1
    n_tok, d = x.shape[1], x.shape[2]
    n_ctx = ctx.shape[1]
    assert n_ctx == ROW_TILE and n_tok % ROW_TILE == 0 and d == D_MODEL
    depth = w_ada.shape[0]

    xs = jnp.concatenate([ctx[0], x[0]], axis=0)
    cc = jnp.zeros((16, d), F32).at[0].set(c_ctx).at[1].set(c[0])
    mods = _ada_call(cc, w_ada, b_ada).reshape(depth, 16, N_MOD, 1, d)

    rope_c, rope_s = _rope_tables(n_ctx, n_tok)
    e_mat = _ssd_expand_matrix()
    conv_w = jnp.pad(ssd_conv_w, ((0, 0), (0, 8 - C_CONV), (0, 0)))
    conv_b = ssd_conv_b.reshape(depth, 1, C_XBC)
    d_e = jnp.repeat(ssd_d, C_HEAD_DIM, axis=-1).reshape(depth, 1, C_INNER)
    wg_p = jnp.zeros((depth, 2, 128, B_QK_WIDTH), F32)
    wg_p = wg_p.at[:, 0, 0:B_RANK].set(gla_w_gate[:, 0]).at[:, 1, B_RANK:2 * B_RANK].set(gla_w_gate[:, 1])
    bg = gla_b_gate.reshape(depth, 2, 1, B_QK_WIDTH)

    for l in range(depth):
        h = _norm_mod_call(xs, norm_g, mods, l, 0, 0, 1)
        hf = _ffn_in_call(h, ffn_w_in, l, 0)
        xs = _mm_res_call(hf, (None, None), (l, 0), ffn_w_out, xs, mods, l, 2, 0.5, n_ctx, (384, 256), 512, "ffn_out")

        h = _norm_mod_call(xs, norm_g, mods, l, 1, 3, 4)
        (u_a,) = _proj_call(h, w_in, l, REG_A, 1536)
        u_b, sm = _proj_call(h, w_in, l, REG_B, 512, with_small=True)
        (u_c,) = _proj_call(h, w_in, l, REG_C, 1024)
        q, k, v = _qk_prep_call(u_a, rope_c, rope_s, qk_g, l)
        o_a = _attn_call(q, k, v, n_ctx)
        of_b, ob_b = _gla_call(u_a, sm, wg_p[l], bg[l])
        o_b = _gla_out_call(of_b, ob_b, u_a, gla_norm_g, l)
        xc = _ssd_conv_call(u_b, conv_w, conv_b, l)
        aneg_p = _lane_row(-jnp.exp(ssd_a_log[l].astype(F32)))
        dtb_p = _lane_row(ssd_dt_bias[l])
        yf_s, yb_s = _ssd_call(xc, sm, dtb_p, aneg_p, e_mat)
        o_s = _ssd_out_call(yf_s, yb_s, xc, u_b, d_e, ssd_norm_g, l)
        m = _merge_call(o_a, o_b, o_s, u_c, w_br, l)
        xs = _mm_res_call(m, (None,), (l,), w_out, xs, mods, l, 5, 1.0, n_ctx, (768, 512, 256), 1024, "mix_out")

        h = _norm_mod_call(xs, norm_g, mods, l, 2, 6, 7)
        hf = _ffn_in_call(h, ffn_w_in, l, 1)
        xs = _mm_res_call(hf, (None, None), (l, 1), ffn_w_out, xs, mods, l, 8, 0.5, n_ctx, (384, 256), 512, "ffn_out")

    return xs[n_ctx:][None]
```

```python
import functools
import math

import jax
import jax.numpy as jnp
from jax import lax
from jax.experimental import pallas as pl
from jax.experimental.pallas import tpu as pltpu

F32 = jnp.float32
BF16 = jnp.bfloat16

D_MODEL = 2048
DEPTH = 4
GRID_W = 64
EPS = 1e-6
N_MOD = 9
D_FF = 5632

A_HEADS = 8
A_KV_HEADS = 2
A_HEAD_DIM = 128
ROPE_THETA = 10000.0
A_WIDTH = A_HEADS * A_HEAD_DIM
A_KV_WIDTH = A_KV_HEADS * A_HEAD_DIM

B_HEADS = 4
B_DK = 128
B_DV = 256
B_RANK = 16
B_TAU = 16.0
B_CHUNK = 64
B_QK_WIDTH = B_HEADS * B_DK
B_WIDTH = B_HEADS * B_DV

C_HEADS = 16
C_HEAD_DIM = 64
C_GROUPS = 2
C_STATE = 128
C_CONV = 5
C_CHUNK = 128
C_INNER = C_HEADS * C_HEAD_DIM
C_XBC = C_INNER + 2 * C_GROUPS * C_STATE
N_BRANCH = 3

W_AKV = 1024
W_BQ = 1536
W_BK = 2048
W_BV = 2560
W_BR = 3584
REG_A = (0, 4608)
REG_B = (4640, 2560)
REG_C = (7232, 6144)
SIDE_BLOCKS = (4608, 2560)
SMALL_DT = 32

ROW_TILE = 256
VMEM_LIMIT = 56 * 1024 * 1024


def _cparams(n_axes):
    return pltpu.CompilerParams(dimension_semantics=("arbitrary",) * n_axes, vmem_limit_bytes=VMEM_LIMIT)


def _pick_tile(total, candidates):
    for c in candidates:
        if total % c == 0:
            return c
    raise ValueError(f"no tile for {total}")


def _silu(x):
    return x * (1.0 / (1.0 + jnp.exp(-x)))


def _sigmoid(x):
    return 1.0 / (1.0 + jnp.exp(-x))


def _split3(x):
    hi = x.astype(BF16)
    r1 = x - hi.astype(F32)
    mid = r1.astype(BF16)
    lo = (r1 - mid.astype(F32)).astype(BF16)
    return hi, mid, lo


def _split2(x):
    hi = x.astype(BF16)
    return hi, (x - hi.astype(F32)).astype(BF16)


def _dot(a, b):
    return jnp.dot(a, b, preferred_element_type=F32)


def _dot_exact_lhs(sel, x):
    hi, mid, lo = _split3(x)
    return _dot(sel, hi) + _dot(sel, mid) + _dot(sel, lo)


def _dot_exact_rhs(x, sel):
    hi, mid, lo = _split3(x)
    return _dot(hi, sel) + _dot(mid, sel) + _dot(lo, sel)


def _ada_kernel(c_ref, w_ref, b_ref, o_ref):
    a = _silu(c_ref[...]).astype(BF16)
    o_ref[...] = _dot(a, w_ref[...].astype(BF16)) + b_ref[...]


def _ada_call(cc, w_ada, b_ada):
    depth, d, n = w_ada.shape
    tn = 1024
    return pl.pallas_call(
        _ada_kernel,
        grid=(depth, n // tn),
        in_specs=[
            pl.BlockSpec((16, d), lambda l, j: (0, 0)),
            pl.BlockSpec((None, d, tn), lambda l, j: (l, 0, j)),
            pl.BlockSpec((None, 1, tn), lambda l, j: (l, 0, j)),
        ],
        out_specs=pl.BlockSpec((None, 16, tn), lambda l, j: (l, 0, j)),
        out_shape=jax.ShapeDtypeStruct((depth, 16, n), F32),
        compiler_params=_cparams(2),
        name="ada_mod",
    )(cc, w_ada, b_ada.reshape(depth, 1, n))


def _norm_mod_kernel(x_ref, g_ref, sh_ref, sc_ref, o_ref):
    x = x_ref[...]
    y = x * lax.rsqrt(jnp.mean(x * x, axis=-1, keepdims=True) + EPS) * g_ref[...]
    o_ref[...] = (y * (1.0 + sc_ref[...]) + sh_ref[...]).astype(o_ref.dtype)


def _norm_mod_call(xs, norm_g, mods, l, gi, shift_i, scale_i):
    t, d = xs.shape
    tm = ROW_TILE
    row = lambda i: jnp.minimum(i, 1)
    return pl.pallas_call(
        _norm_mod_kernel,
        grid=(t // tm,),
        in_specs=[
            pl.BlockSpec((tm, d), lambda i: (i, 0)),
            pl.BlockSpec((None, None, 1, d), lambda i: (l, gi, 0, 0)),
            pl.BlockSpec((None, None, None, 1, d), lambda i: (l, row(i), shift_i, 0, 0)),
            pl.BlockSpec((None, None, None, 1, d), lambda i: (l, row(i), scale_i, 0, 0)),
        ],
        out_specs=pl.BlockSpec((tm, d), lambda i: (i, 0)),
        out_shape=jax.ShapeDtypeStruct((t, d), BF16),
        compiler_params=_cparams(1),
        name="norm_mod",
    )(xs, norm_g.reshape(norm_g.shape[0], 3, 1, d), mods, mods)


def _ffn_in_kernel(a_ref, wa_ref, wu_ref, o_ref, wa_bf, wu_bf):
    @pl.when(pl.program_id(1) == 0)
    def _():
        wa_bf[...] = wa_ref[...].astype(BF16)
        wu_bf[...] = wu_ref[...].astype(BF16)

    a = a_ref[...]
    half = o_ref.shape[1] // 2
    for cs in (slice(0, half), slice(half, 2 * half)):
        ga = _dot(a, wa_bf[:, cs])
        gu = _dot(a, wu_bf[:, cs])
        o_ref[:, cs] = (_silu(ga) * gu).astype(o_ref.dtype)


def _ffn_in_call(h, ffn_w_in, l, f):
    t, d = h.shape
    tn = 512
    tm = _pick_tile(t, (768, 512, 256))
    nj = D_FF // tn
    return pl.pallas_call(
        _ffn_in_kernel,
        grid=(nj, t // tm),
        in_specs=[
            pl.BlockSpec((tm, d), lambda j, i: (i, 0)),
            pl.BlockSpec((None, None, d, tn), lambda j, i: (l, f, 0, j)),
            pl.BlockSpec((None, None, d, tn), lambda j, i: (l, f, 0, j + nj)),
        ],
        out_specs=pl.BlockSpec((tm, tn), lambda j, i: (i, j)),
        out_shape=jax.ShapeDtypeStruct((t, D_FF), BF16),
        scratch_shapes=[pltpu.VMEM((d, tn), BF16), pltpu.VMEM((d, tn), BF16)],
        compiler_params=_cparams(2),
        name="ffn_in",
    )(h, ffn_w_in, ffn_w_in)


def _mm_res_kernel(a_ref, w_ref, res_ref, gate_ref, o_ref, w_bf, *, factor, tm, n_ctx):
    @pl.when(pl.program_id(1) == 0)
    def _():
        w_bf[...] = w_ref[...].astype(BF16)

    acc = _dot(a_ref[...], w_bf[...])
    rows = pl.program_id(1) * tm + lax.broadcasted_iota(jnp.int32, (tm, 1), 0)
    gate = jnp.where(rows < n_ctx, gate_ref[0], gate_ref[1])
    if factor != 1.0:
        gate = factor * gate
    o_ref[...] = res_ref[...] + gate * acc


def _mm_res_call(a, w_spec_shape, w_index, w, xs, mods, l, gate_i, factor, n_ctx, tm_cands, tn, name):
    t, k = a.shape
    d = xs.shape[1]
    tm = _pick_tile(t, tm_cands)
    return pl.pallas_call(
        functools.partial(_mm_res_kernel, factor=factor, tm=tm, n_ctx=n_ctx),
        grid=(d // tn, t // tm),
        in_specs=[
            pl.BlockSpec((tm, k), lambda j, i: (i, 0)),
            pl.BlockSpec(w_spec_shape + (k, tn), lambda j, i: w_index + (0, j)),
            pl.BlockSpec((tm, tn), lambda j, i: (i, j)),
            pl.BlockSpec((None, 2, None, 1, tn), lambda j, i: (l, 0, gate_i, 0, j)),
        ],
        out_specs=pl.BlockSpec((tm, tn), lambda j, i: (i, j)),
        out_shape=jax.ShapeDtypeStruct((t, d), F32),
        scratch_shapes=[pltpu.VMEM((k, tn), BF16)],
        compiler_params=_cparams(2),
        name=name,
    )(a, w, xs, mods)


def _transpose_to_bf16(w_ref, w_bf):
    n = w_ref.shape[0]
    step = 256 if n % 256 == 0 else 128
    for r in range(0, n, step):
        w_bf[:, r:r + step] = w_ref[r:r + step, :].T.astype(BF16)


def _proj_kernel(a_ref, w_ref, *rest, with_small):
    if with_small:
        wsm_ref, o_ref, sm_ref, w_bf, wsm_bf = rest
    else:
        o_ref, w_bf = rest

    @pl.when(pl.program_id(1) == 0)
    def _():
        _transpose_to_bf16(w_ref, w_bf)
        if with_small:
            _transpose_to_bf16(wsm_ref, wsm_bf)

    a = a_ref[...]
    o_ref[...] = _dot(a, w_bf[...])
    if with_small:
        sm_ref[...] = _dot(a, wsm_bf[...])


def _proj_call(h, w_in_t, l, start, width, tn, small=None):
    t, d = h.shape
    tm = _pick_tile(t, (768, 512, 256))
    nj = width // tn
    assert width % tn == 0 and start % 8 == 0
    in_specs = [
        pl.BlockSpec((tm, d), lambda j, i: (i, 0)),
        pl.BlockSpec((None, pl.Element(tn), pl.Element(d)), lambda j, i: (l, pl.multiple_of(start + j * tn, 8), 0)),
    ]
    out_specs = [pl.BlockSpec((tm, tn), lambda j, i: (i, j))]
    out_shape = [jax.ShapeDtypeStruct((t, width), F32)]
    scratch = [pltpu.VMEM((d, tn), BF16)]
    args = [h, w_in_t]
    if small is not None:
        first, stride = small
        in_specs.append(pl.BlockSpec((None, pl.Element(128), pl.Element(d)),
                                     lambda j, i: (l, pl.multiple_of(first + j * stride, 8), 0)))
        args.append(w_in_t)
        out_specs.append(pl.BlockSpec((tm, 128), lambda j, i: (i, j)))
        out_shape.append(jax.ShapeDtypeStruct((t, nj * 128), F32))
        scratch.append(pltpu.VMEM((d, 128), BF16))
    return pl.pallas_call(
        functools.partial(_proj_kernel, with_small=small is not None),
        grid=(nj, t // tm),
        in_specs=in_specs,
        out_specs=out_specs,
        out_shape=out_shape,
        scratch_shapes=scratch,
        compiler_params=_cparams(2),
        name="in_proj",
    )(*args)


def _qk_prep_kernel(q_in, kv_in, cos_ref, sin_ref, g_ref, q_ref, k_ref, v_ref):
    tm = q_in.shape[0]
    cs = cos_ref[...]
    sn = sin_ref[...]
    lane = lax.broadcasted_iota(jnp.int32, (tm, A_HEAD_DIM), 1)
    first = (lane % 64) < 32
    c2 = (A_HEAD_DIM ** -0.5) * math.log2(math.e)

    def norm_rope(x, g):
        y = x * lax.rsqrt(jnp.mean(x * x, axis=-1, keepdims=True) + EPS) * g
        partner = jnp.where(first, pltpu.roll(y, 96, 1), pltpu.roll(y, 32, 1))
        return y * cs + partner * sn

    for h in range(A_HEADS):
        q_ref[h] = (norm_rope(q_in[:, h * 128:(h + 1) * 128], g_ref[0:1, :]) * c2).astype(BF16)
    for h in range(A_KV_HEADS):
        k_ref[h] = norm_rope(kv_in[:, h * 128:(h + 1) * 128], g_ref[1:2, :]).astype(BF16)
        v_ref[h, :, 0:A_HEAD_DIM] = kv_in[:, A_KV_WIDTH + h * 128:A_KV_WIDTH + (h + 1) * 128].astype(BF16)
        v_ref[h, :, A_HEAD_DIM:2 * A_HEAD_DIM] = jnp.ones((tm, A_HEAD_DIM), BF16)


def _qk_prep_call(u_a, rope_c, rope_s, qk_g, l):
    t = u_a.shape[0]
    tm = ROW_TILE
    return pl.pallas_call(
        _qk_prep_kernel,
        grid=(t // tm,),
        in_specs=[
            pl.BlockSpec((tm, A_WIDTH), lambda i: (i, 0)),
            pl.BlockSpec((tm, 2 * A_KV_WIDTH), lambda i: (i, W_AKV // (2 * A_KV_WIDTH))),
            pl.BlockSpec((tm, A_HEAD_DIM), lambda i: (i, 0)),
            pl.BlockSpec((tm, A_HEAD_DIM), lambda i: (i, 0)),
            pl.BlockSpec((None, 2, A_HEAD_DIM), lambda i: (l, 0, 0)),
        ],
        out_specs=[
            pl.BlockSpec((A_HEADS, tm, A_HEAD_DIM), lambda i: (0, i, 0)),
            pl.BlockSpec((A_KV_HEADS, tm, A_HEAD_DIM), lambda i: (0, i, 0)),
            pl.BlockSpec((A_KV_HEADS, tm, 2 * A_HEAD_DIM), lambda i: (0, i, 0)),
        ],
        out_shape=[
            jax.ShapeDtypeStruct((A_HEADS, t, A_HEAD_DIM), BF16),
            jax.ShapeDtypeStruct((A_KV_HEADS, t, A_HEAD_DIM), BF16),
            jax.ShapeDtypeStruct((A_KV_HEADS, t, 2 * A_HEAD_DIM), BF16),
        ],
        compiler_params=_cparams(1),
        name="qk_prep",
    )(u_a, u_a, rope_c, rope_s, qk_g)


def _attn_kernel(q_ref, k_ref, v_ref, o_ref, m_sc, acc_sc, sa_sc, sb_sc, pa_sc, pb_sc, *, tq, n_ctx, t_all, ck):
    grp = A_HEADS // A_KV_HEADS
    rows = grp * tq
    dh = A_HEAD_DIM
    rb = 64
    n_chunks = t_all // ck
    q = q_ref[...].reshape(rows, dh)
    contract_last = (((1,), (1,)), ((), ()))

    def write_out(out):
        for h in range(grp):
            o_ref[:, h * dh:(h + 1) * dh] = out[h * tq:(h + 1) * tq].astype(o_ref.dtype)

    def chunk_rows(c):
        return pl.ds(c * ck if isinstance(c, int) else pl.multiple_of(c * ck, ck), ck)

    def qk(s_sc, c):
        s_sc[...] = lax.dot_general(q, k_ref[chunk_rows(c), :], contract_last, preferred_element_type=F32)

    def softmax_pv(s_sc, p_sc, c):
        for r in range(rows // rb):
            rs = slice(r * rb, (r + 1) * rb)
            s = s_sc[rs, :]
            m_prev = m_sc[rs, :]
            m_next = jnp.maximum(m_prev, jnp.max(s, axis=-1, keepdims=True))
            p_sc[rs, :] = jnp.exp2(s - jnp.tile(m_next, (1, ck // dh))).astype(BF16)
            acc_sc[rs, :] = jnp.tile(jnp.exp2(m_prev - m_next), (1, 2)) * acc_sc[rs, :]
            m_sc[rs, :] = m_next
        acc_sc[...] += _dot(p_sc[...], v_ref[chunk_rows(c), :])

    i = pl.program_id(1)

    @pl.when(i == 0)
    def _():
        s = lax.dot_general(q, k_ref[0:n_ctx, :], contract_last, preferred_element_type=F32)
        p = jnp.exp2(s - jnp.max(s, axis=-1, keepdims=True)).astype(BF16)
        acc = _dot(p, v_ref[0:n_ctx, :])
        write_out(acc[:, 0:dh] / acc[:, dh:2 * dh])

    @pl.when(i > 0)
    def _():
        m_sc[...] = jnp.full((rows, dh), -jnp.inf, F32)
        acc_sc[...] = jnp.zeros((rows, 2 * dh), F32)
        qk(sa_sc, 0)

        def body(it, carry):
            c = 2 * it
            qk(sb_sc, c + 1)
            softmax_pv(sa_sc, pa_sc, c)
            qk(sa_sc, c + 2)
            softmax_pv(sb_sc, pb_sc, c + 1)
            return carry

        lax.fori_loop(0, (n_chunks - 1) // 2, body, 0)
        if n_chunks % 2 == 0:
            qk(sb_sc, n_chunks - 1)
            softmax_pv(sa_sc, pa_sc, n_chunks - 2)
            softmax_pv(sb_sc, pb_sc, n_chunks - 1)
        else:
            softmax_pv(sa_sc, pa_sc, n_chunks - 1)
        write_out(acc_sc[:, 0:dh] / acc_sc[:, dh:2 * dh])


def _attn_call(q, k, v, n_ctx):
    _, t, dh = q.shape
    tq = ROW_TILE
    grp = A_HEADS // A_KV_HEADS
    ck = _pick_tile(t, (768, 512, 256))
    return pl.pallas_call(
        functools.partial(_attn_kernel, tq=tq, n_ctx=n_ctx, t_all=t, ck=ck),
        grid=(A_KV_HEADS, t // tq),
        in_specs=[
            pl.BlockSpec((grp, tq, dh), lambda g, i: (g, i, 0)),
            pl.BlockSpec((None, t, dh), lambda g, i: (g, 0, 0)),
            pl.BlockSpec((None, t, 2 * dh), lambda g, i: (g, 0, 0)),
        ],
        out_specs=pl.BlockSpec((tq, grp * dh), lambda g, i: (i, g)),
        out_shape=jax.ShapeDtypeStruct((t, A_WIDTH), BF16),
        scratch_shapes=[
            pltpu.VMEM((grp * tq, dh), F32),
            pltpu.VMEM((grp * tq, 2 * dh), F32),
            pltpu.VMEM((grp * tq, ck), F32),
            pltpu.VMEM((grp * tq, ck), F32),
            pltpu.VMEM((grp * tq, ck), BF16),
            pltpu.VMEM((grp * tq, ck), BF16),
        ],
        compiler_params=_cparams(2),
        name="attn",
    )(q, k, v)


def _bwd_block(i, nb):
    return jnp.where(i == 0, 0, nb - i)


def _gla_dir(q_ref, k_ref, v01_ref, v23_ref, sm_ref, wg_ref, bg_ref, o_ref, st_ref, reverse):
    tm = q_ref.shape[0]
    nch = tm // B_CHUNK
    r = lax.broadcasted_iota(jnp.int32, (tm, tm), 0)
    c = lax.broadcasted_iota(jnp.int32, (tm, tm), 1)
    same = (r // B_CHUNK) == (c // B_CHUNK)
    tri = same & ((c >= r) if reverse else (c <= r))
    tri_bf = tri.astype(BF16)

    g_hi, g_lo = _split2(sm_ref[...])
    w_hi, w_lo = _split2(wg_ref[...])
    pre = _dot(g_hi, w_hi) + _dot(g_hi, w_lo) + _dot(g_lo, w_hi) + bg_ref[...]
    la = (jnp.minimum(pre, 0.0) - jnp.log1p(jnp.exp(-jnp.abs(pre)))) / B_TAU
    b = _dot_exact_lhs(tri_bf, la)
    ends = [(j * B_CHUNK) if reverse else ((j + 1) * B_CHUNK - 1) for j in range(nch)]
    tot = jnp.concatenate([jnp.broadcast_to(b[e:e + 1, :], (B_CHUNK, B_QK_WIDTH)) for e in ends], axis=0)
    eb = jnp.exp(b)
    qf = (q_ref[...] * (B_DK ** -0.5)) * eb
    kf = k_ref[...] * jnp.exp(-b)
    k_end = k_ref[...] * jnp.exp(tot - b)
    dec = jnp.exp(tot)
    v_pairs = (v01_ref[...].astype(BF16), v23_ref[...].astype(BF16))
    qf_bf = qf.astype(BF16)
    kf_bf = kf.astype(BF16)
    ke_bf = k_end.astype(BF16)

    order = range(nch - 1, -1, -1) if reverse else range(nch)
    for h in range(B_HEADS):
        ks = slice(h * B_DK, (h + 1) * B_DK)
        vs = slice(h * B_DV, (h + 1) * B_DV)
        att = lax.dot_general(qf_bf[:, ks], kf_bf[:, ks], (((1,), (1,)), ((), ())), preferred_element_type=F32)
        att = jnp.where(tri, att, 0.0).astype(BF16)
        v = v_pairs[h // 2][:, (h % 2) * B_DV:(h % 2 + 1) * B_DV]
        o_intra = _dot(att, v)
        for j in order:
            rs = slice(j * B_CHUNK, (j + 1) * B_CHUNK)
            st = st_ref[h]
            o_inter = lax.dot_general(qf_bf[rs, ks], st.astype(BF16), (((1,), (1,)), ((), ())),
                                      preferred_element_type=F32)
            o_ref[rs, vs] = o_intra[rs] + o_inter
            ut = lax.dot_general(v[rs], ke_bf[rs, ks], (((0,), (0,)), ((), ())), preferred_element_type=F32)
            st_ref[h] = dec[j * B_CHUNK:j * B_CHUNK + 1, ks] * st + ut


def _gla_kernel(qf_ref, kf_ref, vf01_ref, vf23_ref, sf_ref, qb_ref, kb_ref, vb01_ref, vb23_ref, sb_ref, wg_ref, bg_ref,
                of_ref, ob_ref, stf_ref, stb_ref):
    @pl.when(pl.program_id(0) == 0)
    def _():
        stf_ref[...] = jnp.zeros_like(stf_ref)
        stb_ref[...] = jnp.zeros_like(stb_ref)

    _gla_dir(qf_ref, kf_ref, vf01_ref, vf23_ref, sf_ref, wg_ref.at[0], bg_ref.at[0], of_ref, stf_ref, False)
    _gla_dir(qb_ref, kb_ref, vb01_ref, vb23_ref, sb_ref, wg_ref.at[1], bg_ref.at[1], ob_ref, stb_ref, True)


def _gla_call(u_a, sm, wg_p, bg):
    t = u_a.shape[0]
    tm = ROW_TILE
    nb = t // tm
    fwd = lambda i: i
    bwd = lambda i: _bwd_block(i, nb)

    def views(blk):
        return [
            pl.BlockSpec((tm, B_QK_WIDTH), lambda i: (blk(i), W_BQ // B_QK_WIDTH)),
            pl.BlockSpec((tm, B_QK_WIDTH), lambda i: (blk(i), W_BK // B_QK_WIDTH)),
            pl.BlockSpec((tm, 2 * B_DV), lambda i: (blk(i), W_BV // (2 * B_DV))),
            pl.BlockSpec((tm, 2 * B_DV), lambda i: (blk(i), W_BV // (2 * B_DV) + 1)),
            pl.BlockSpec((tm, 128), lambda i: (blk(i), 0)),
        ]

    return pl.pallas_call(
        _gla_kernel,
        grid=(nb,),
        in_specs=views(fwd) + views(bwd) + [
            pl.BlockSpec((2, 128, B_QK_WIDTH), lambda i: (0, 0, 0)),
            pl.BlockSpec((2, 1, B_QK_WIDTH), lambda i: (0, 0, 0)),
        ],
        out_specs=[
            pl.BlockSpec((tm, B_WIDTH), lambda i: (fwd(i), 0)),
            pl.BlockSpec((tm, B_WIDTH), lambda i: (bwd(i), 0)),
        ],
        out_shape=[jax.ShapeDtypeStruct((t, B_WIDTH), F32), jax.ShapeDtypeStruct((t, B_WIDTH), F32)],
        scratch_shapes=[pltpu.VMEM((B_HEADS, B_DV, B_DK), F32), pltpu.VMEM((B_HEADS, B_DV, B_DK), F32)],
        compiler_params=_cparams(1),
        name="gla_scan",
    )(u_a, u_a, u_a, u_a, sm, u_a, u_a, u_a, u_a, sm, wg_p, bg)


def _gla_out_kernel(of_ref, ob_ref, r01_ref, r23_ref, g_ref, y_ref):
    for h in range(B_HEADS):
        vs = slice(h * B_DV, (h + 1) * B_DV)
        r_ref = r01_ref if h < 2 else r23_ref
        o = of_ref[:, vs] + ob_ref[:, vs]
        y = o * lax.rsqrt(jnp.mean(o * o, axis=-1, keepdims=True) + EPS) * g_ref[...]
        y_ref[:, vs] = (y * _silu(r_ref[:, (h % 2) * B_DV:(h % 2 + 1) * B_DV])).astype(y_ref.dtype)


def _gla_out_call(o_f, o_b, u_a, gla_norm_g, l):
    t = u_a.shape[0]
    tm = ROW_TILE
    return pl.pallas_call(
        _gla_out_kernel,
        grid=(t // tm,),
        in_specs=[
            pl.BlockSpec((tm, B_WIDTH), lambda i: (i, 0)),
            pl.BlockSpec((tm, B_WIDTH), lambda i: (i, 0)),
            pl.BlockSpec((tm, 2 * B_DV), lambda i: (i, W_BR // (2 * B_DV))),
            pl.BlockSpec((tm, 2 * B_DV), lambda i: (i, W_BR // (2 * B_DV) + 1)),
            pl.BlockSpec((None, 1, B_DV), lambda i: (l, 0, 0)),
        ],
        out_specs=pl.BlockSpec((tm, B_WIDTH), lambda i: (i, 0)),
        out_shape=jax.ShapeDtypeStruct((t, B_WIDTH), BF16),
        compiler_params=_cparams(1),
        name="gla_out",
    )(o_f, o_b, u_a, u_a, gla_norm_g.reshape(gla_norm_g.shape[0], 1, B_DV))


def _ssd_conv_kernel(prev_ref, cur_ref, next_ref, w_ref, b_ref, o_ref, *, nb):
    i = pl.program_id(0)
    tm = cur_ref.shape[0]
    prev_ok = (i >= 2).astype(F32)
    next_ok = ((i >= 1) & (i < nb - 1)).astype(F32)
    ext = jnp.concatenate([prev_ref[...] * prev_ok, cur_ref[...], next_ref[...] * next_ok], axis=0)
    n_ext = tm + 16
    acc = jnp.zeros((tm, C_XBC), F32) + b_ref[...]
    for j in range(C_CONV):
        shift = (C_CONV // 2 - j) % n_ext
        sh = ext if shift == 0 else pltpu.roll(ext, shift, 0)
        acc = acc + sh[8:8 + tm, :] * w_ref[j:j + 1, :]
    o_ref[...] = _silu(acc)


def _ssd_conv_call(u, conv_w, conv_b, l):
    t = u.shape[0]
    tm = ROW_TILE
    nb = t // tm
    r8 = tm // 8
    cb = 0
    return pl.pallas_call(
        functools.partial(_ssd_conv_kernel, nb=nb),
        grid=(nb,),
        in_specs=[
            pl.BlockSpec((8, C_XBC), lambda i: (jnp.maximum(i * r8 - 1, 0), cb)),
            pl.BlockSpec((tm, C_XBC), lambda i: (i, cb)),
            pl.BlockSpec((8, C_XBC), lambda i: (jnp.minimum((i + 1) * r8, nb * r8 - 1), cb)),
            pl.BlockSpec((None, 8, C_XBC), lambda i: (l, 0, 0)),
            pl.BlockSpec((None, 1, C_XBC), lambda i: (l, 0, 0)),
        ],
        out_specs=pl.BlockSpec((tm, C_XBC), lambda i: (i, 0)),
        out_shape=jax.ShapeDtypeStruct((t, C_XBC), F32),
        compiler_params=_cparams(1),
        name="ssd_conv",
    )(u, u, u, conv_w, conv_b)


def _ssd_dir(xc_ref, sm_ref, dtb_ref, aneg_ref, e_ref, y_ref, st_ref, d, reverse):
    tm = xc_ref.shape[0]
    nch = tm // C_CHUNK
    cc = C_CHUNK
    r = lax.broadcasted_iota(jnp.int32, (cc, cc), 0)
    c = lax.broadcasted_iota(jnp.int32, (cc, cc), 1)
    tri = (c >= r) if reverse else (c <= r)
    tri_bf = tri.astype(BF16)
    lane = lax.broadcasted_iota(jnp.int32, (cc, 128), 1)
    left = lane < C_HEAD_DIM
    e_bf = e_ref[...]

    order = range(nch - 1, -1, -1) if reverse else range(nch)
    for j in order:
        rs = slice(j * cc, (j + 1) * cc)
        x = xc_ref[rs, 0:C_INNER]
        dt_raw = sm_ref[rs, :] + dtb_ref[...]
        dt = jnp.maximum(dt_raw, 0.0) + jnp.log1p(jnp.exp(-jnp.abs(dt_raw)))
        a = dt * aneg_ref[...]
        cum = _dot_exact_lhs(tri_bf, a)
        cum_t = cum.T
        dt_hi, dt_lo = _split2(dt)
        dt_e = _dot(dt_hi, e_bf) + _dot(dt_lo, e_bf)
        cum_e = _dot_exact_rhs(cum, e_bf)
        end = 0 if reverse else cc - 1
        tot_e = cum_e[end:end + 1, :]
        xd = x * dt_e
        xw_bf = (xd * jnp.exp(tot_e - cum_e)).astype(BF16)
        xd_bf = xd.astype(BF16)
        ec_e = jnp.exp(cum_e)
        dec_e = jnp.exp(tot_e)
        for g in range(C_GROUPS):
            bm = xc_ref[rs, C_INNER + g * C_STATE:C_INNER + (g + 1) * C_STATE]
            cm = xc_ref[rs, C_INNER + (C_GROUPS + g) * C_STATE:C_INNER + (C_GROUPS + g + 1) * C_STATE]
            bm_bf = bm.astype(BF16)
            cm_bf = cm.astype(BF16)
            cb = lax.dot_general(cm_bf, bm_bf, (((1,), (1,)), ((), ())), preferred_element_type=F32)
            bt_bf = bm.T.astype(BF16)
            for pr in range(C_HEADS // C_GROUPS // 2):
                pair = g * (C_HEADS // C_GROUPS // 2) + pr
                ls = slice(pair * 128, (pair + 1) * 128)
                y_pair = None
                for half in range(2):
                    hl = SMALL_DT + 16 * d + 2 * pair + half
                    col = jnp.broadcast_to(cum[:, hl:hl + 1], (cc, cc))
                    row = jnp.broadcast_to(cum_t[hl:hl + 1, :], (cc, cc))
                    w = (cb * jnp.where(tri, jnp.exp(col - row), 0.0)).astype(BF16)
                    xm = jnp.where(left if half == 0 else ~left, xd_bf[:, ls], jnp.zeros((), BF16))
                    part = _dot(w, xm)
                    y_pair = part if y_pair is None else y_pair + part
                st = st_ref[pair]
                y_ref[rs, ls] = y_pair + _dot(cm_bf, st.astype(BF16)) * ec_e[:, ls]
                st_ref[pair] = dec_e[:, ls] * st + _dot(bt_bf, xw_bf[:, ls])


def _ssd_kernel(xf_ref, sf_ref, xb_ref, sb_ref, dtb_ref, aneg_ref, e_ref, yf_ref, yb_ref, stf_ref, stb_ref):
    @pl.when(pl.program_id(0) == 0)
    def _():
        stf_ref[...] = jnp.zeros_like(stf_ref)
        stb_ref[...] = jnp.zeros_like(stb_ref)

    _ssd_dir(xf_ref, sf_ref, dtb_ref, aneg_ref, e_ref.at[0], yf_ref, stf_ref, 0, False)
    _ssd_dir(xb_ref, sb_ref, dtb_ref, aneg_ref, e_ref.at[1], yb_ref, stb_ref, 1, True)


def _ssd_call(xc, sm, dtb_p, aneg_p, e_mat):
    t = xc.shape[0]
    tm = ROW_TILE
    nb = t // tm
    dt_blk = sm.shape[1] // 128 - 1
    fwd = lambda i: i
    bwd = lambda i: _bwd_block(i, nb)

    def views(blk):
        return [
            pl.BlockSpec((tm, C_XBC), lambda i: (blk(i), 0)),
            pl.BlockSpec((tm, 128), lambda i: (blk(i), dt_blk)),
        ]

    return pl.pallas_call(
        _ssd_kernel,
        grid=(nb,),
        in_specs=views(fwd) + views(bwd) + [
            pl.BlockSpec((1, 128), lambda i: (0, 0)),
            pl.BlockSpec((1, 128), lambda i: (0, 0)),
            pl.BlockSpec((2, 128, C_INNER), lambda i: (0, 0, 0)),
        ],
        out_specs=[
            pl.BlockSpec((tm, C_INNER), lambda i: (fwd(i), 0)),
            pl.BlockSpec((tm, C_INNER), lambda i: (bwd(i), 0)),
        ],
        out_shape=[jax.ShapeDtypeStruct((t, C_INNER), F32), jax.ShapeDtypeStruct((t, C_INNER), F32)],
        scratch_shapes=[pltpu.VMEM((C_HEADS // 2, C_STATE, 128), F32), pltpu.VMEM((C_HEADS // 2, C_STATE, 128), F32)],
        compiler_params=_cparams(1),
        name="ssd_scan",
    )(xc, sm, xc, sm, dtb_p, aneg_p, e_mat)


def _ssd_out_kernel(yf_ref, yb_ref, xc_ref, z0_ref, z1_ref, d_ref, g_ref, y_ref):
    gw = C_INNER // C_GROUPS
    for g, z_ref in enumerate((z0_ref, z1_ref)):
        ls = slice(g * gw, (g + 1) * gw)
        y = yf_ref[:, ls] + yb_ref[:, ls] + d_ref[:, ls] * xc_ref[:, ls]
        y = y * _silu(z_ref[...])
        y = y * lax.rsqrt(jnp.mean(y * y, axis=-1, keepdims=True) + EPS) * g_ref[:, ls]
        y_ref[:, ls] = y.astype(y_ref.dtype)


def _ssd_out_call(y_f, y_b, xc, u_b, d_e, ssd_norm_g, l):
    t = xc.shape[0]
    tm = ROW_TILE
    gw = C_INNER // C_GROUPS
    return pl.pallas_call(
        _ssd_out_kernel,
        grid=(t // tm,),
        in_specs=[
            pl.BlockSpec((tm, C_INNER), lambda i: (i, 0)),
            pl.BlockSpec((tm, C_INNER), lambda i: (i, 0)),
            pl.BlockSpec((tm, C_INNER), lambda i: (i, 0)),
            pl.BlockSpec((tm, gw), lambda i: (i, C_XBC // gw)),
            pl.BlockSpec((tm, gw), lambda i: (i, C_XBC // gw + 1)),
            pl.BlockSpec((None, 1, C_INNER), lambda i: (l, 0, 0)),
            pl.BlockSpec((None, 1, C_INNER), lambda i: (l, 0, 0)),
        ],
        out_specs=pl.BlockSpec((tm, C_INNER), lambda i: (i, 0)),
        out_shape=jax.ShapeDtypeStruct((t, C_INNER), BF16),
        compiler_params=_cparams(1),
        name="ssd_out",
    )(y_f, y_b, xc, u_b, u_b, d_e, ssd_norm_g.reshape(ssd_norm_g.shape[0], 1, C_INNER))


def _merge_kernel(oa_ref, ob_ref, os_ref, wa_ref, wb_ref, ws_ref, ga_ref, gb_ref, gs_ref, o_ref, w_bf):
    @pl.when(pl.program_id(1) == 0)
    def _():
        w_bf[0] = wa_ref[...].astype(BF16)
        w_bf[1] = wb_ref[...].astype(BF16)
        w_bf[2] = ws_ref[...].astype(BF16)

    m = _sigmoid(ga_ref[...]) * _dot(oa_ref[...], w_bf[0])
    m = m + _sigmoid(gb_ref[...]) * _dot(ob_ref[...], w_bf[1])
    m = m + _sigmoid(gs_ref[...]) * _dot(os_ref[...], w_bf[2])
    o_ref[...] = m.astype(o_ref.dtype)


def _merge_call(o_a, o_b, o_s, u, w_br, l):
    t = u.shape[0]
    tn = 512
    tm = _pick_tile(t, (768, 512, 256))
    k = A_WIDTH
    gb = 0
    nd = D_MODEL // tn
    a_spec = pl.BlockSpec((tm, k), lambda j, i: (i, 0))
    return pl.pallas_call(
        _merge_kernel,
        grid=(nd, t // tm),
        in_specs=[
            a_spec, a_spec, a_spec,
            pl.BlockSpec((None, k, tn), lambda j, i: (l, 0, j)),
            pl.BlockSpec((None, k, tn), lambda j, i: (l, 1, j)),
            pl.BlockSpec((None, k, tn), lambda j, i: (l, 2, j)),
            pl.BlockSpec((tm, tn), lambda j, i: (i, gb + j)),
            pl.BlockSpec((tm, tn), lambda j, i: (i, gb + nd + j)),
            pl.BlockSpec((tm, tn), lambda j, i: (i, gb + 2 * nd + j)),
        ],
        out_specs=pl.BlockSpec((tm, tn), lambda j, i: (i, j)),
        out_shape=jax.ShapeDtypeStruct((t, D_MODEL), BF16),
        scratch_shapes=[pltpu.VMEM((N_BRANCH, k, tn), BF16)],
        compiler_params=_cparams(2),
        name="branch_merge",
    )(o_a, o_b, o_s, w_br, w_br, w_br, u, u, u)


def _rope_tables(n_ctx, n_tok):
    rows_n = n_tok // GRID_W
    rows = jnp.repeat(jnp.arange(rows_n), GRID_W)
    cols = jnp.tile(jnp.arange(GRID_W), rows_n)
    pos = jnp.stack([rows, cols], axis=-1).astype(F32)
    half = A_HEAD_DIM // 2
    inv = ROPE_THETA ** (-jnp.arange(0, half, 2, dtype=F32) / half)
    ang = pos[:, :, None] * inv
    cos, sin = jnp.cos(ang), jnp.sin(ang)
    c_tab = jnp.concatenate([cos[:, 0], cos[:, 0], cos[:, 1], cos[:, 1]], axis=-1)
    s_tab = jnp.concatenate([-sin[:, 0], sin[:, 0], -sin[:, 1], sin[:, 1]], axis=-1)
    c_tab = jnp.concatenate([jnp.ones((n_ctx, A_HEAD_DIM), F32), c_tab], axis=0)
    s_tab = jnp.concatenate([jnp.zeros((n_ctx, A_HEAD_DIM), F32), s_tab], axis=0)
    return c_tab, s_tab


def _ssd_expand_matrix():
    lane = jnp.arange(128)[:, None]
    head = jnp.arange(C_INNER)[None, :] // C_HEAD_DIM
    return jnp.stack([(lane == SMALL_DT + 16 * d + head) for d in range(2)]).astype(BF16)


def _lane_row(vals):
    row = jnp.zeros((128,), F32)
    row = lax.dynamic_update_slice(row, vals.reshape(-1).astype(F32), (SMALL_DT,))
    return row.reshape(1, 128)


def kernel(x, c, ctx, c_ctx, w_ada, b_ada, norm_g, ffn_w_in, ffn_w_out, w_in, qk_g, gla_w_gate, gla_b_gate,
           gla_norm_g, ssd_conv_w, ssd_conv_b, ssd_a_log, ssd_dt_bias, ssd_d, ssd_norm_g, w_br, w_out):
    assert x.shape[0] == 1 and ctx.shape[0] == 1
    n_tok, d = x.shape[1], x.shape[2]
    n_ctx = ctx.shape[1]
    assert n_ctx == ROW_TILE and n_tok % ROW_TILE == 0 and d == D_MODEL
    depth = w_ada.shape[0]

    xs = jnp.concatenate([ctx[0], x[0]], axis=0)
    cc = jnp.zeros((16, d), F32).at[0].set(c_ctx).at[1].set(c[0])
    mods = _ada_call(cc, w_ada, b_ada).reshape(depth, 16, N_MOD, 1, d)

    w_in_t = jnp.swapaxes(w_in, 1, 2)
    rope_c, rope_s = _rope_tables(n_ctx, n_tok)
    e_mat = _ssd_expand_matrix()
    conv_w = jnp.pad(ssd_conv_w, ((0, 0), (0, 8 - C_CONV), (0, 0)))
    conv_b = ssd_conv_b.reshape(depth, 1, C_XBC)
    d_e = jnp.repeat(ssd_d, C_HEAD_DIM, axis=-1).reshape(depth, 1, C_INNER)
    wg_p = jnp.zeros((depth, 2, 128, B_QK_WIDTH), F32)
    wg_p = wg_p.at[:, 0, 0:B_RANK].set(gla_w_gate[:, 0]).at[:, 1, B_RANK:2 * B_RANK].set(gla_w_gate[:, 1])
    bg = gla_b_gate.reshape(depth, 2, 1, B_QK_WIDTH)

    for l in range(depth):
        h = _norm_mod_call(xs, norm_g, mods, l, 0, 0, 1)
        hf = _ffn_in_call(h, ffn_w_in, l, 0)
        xs = _mm_res_call(hf, (None, None), (l, 0), ffn_w_out, xs, mods, l, 2, 0.5, n_ctx, (384, 256), 512, "ffn_out")

        h = _norm_mod_call(xs, norm_g, mods, l, 1, 3, 4)
        (u_a,) = _proj_call(h, w_in_t, l, *REG_A, 1536)
        u_b, sm = _proj_call(h, w_in_t, l, *REG_B, 1280, small=SIDE_BLOCKS)
        (u_c,) = _proj_call(h, w_in_t, l, *REG_C, 1536)
        q, k, v = _qk_prep_call(u_a, rope_c, rope_s, qk_g, l)
        o_a = _attn_call(q, k, v, n_ctx)
        of_b, ob_b = _gla_call(u_a, sm, wg_p[l], bg[l])
        o_b = _gla_out_call(of_b, ob_b, u_a, gla_norm_g, l)
        xc = _ssd_conv_call(u_b, conv_w, conv_b, l)
        aneg_p = _lane_row(-jnp.exp(ssd_a_log[l].astype(F32)))
        dtb_p = _lane_row(ssd_dt_bias[l])
        yf_s, yb_s = _ssd_call(xc, sm, dtb_p, aneg_p, e_mat)
        o_s = _ssd_out_call(yf_s, yb_s, xc, u_b, d_e, ssd_norm_g, l)
        m = _merge_call(o_a, o_b, o_s, u_c, w_br, l)
        xs = _mm_res_call(m, (None,), (l,), w_out, xs, mods, l, 5, 1.0, n_ctx, (768, 512, 256), 1024, "mix_out")

        h = _norm_mod_call(xs, norm_g, mods, l, 2, 6, 7)
        hf = _ffn_in_call(h, ffn_w_in, l, 1)
        xs = _mm_res_call(hf, (None, None), (l, 1), ffn_w_out, xs, mods, l, 8, 0.5, n_ctx, (384, 256), 512, "ffn_out")

    return xs[n_ctx:][None]
```
